```python
import jax, jax.numpy as jnp
from jax import lax
import numpy as np

D_MODEL = 1024
BATCH = 8
SEQ = 2048
DEPTH = 2
DEC_BATCH = 128
DEC_SEQ = 8
PAST_LEN = 16384
PAGE_SIZE = 128

N_AB = (DEPTH + 1) // 2
N_C = DEPTH // 2
D_A = D_MODEL // 2
POOL_WINDOWS = (2, 4, 8, 16)
N_POOL_GROUPS = len(POOL_WINDOWS)
POOL_GROUP = D_A // N_POOL_GROUPS
POOL_BUF = max(POOL_WINDOWS) - 1
D_B = D_MODEL // 2
CONV_K = 31
CONV_BUF = CONV_K - 1
D_C = D_MODEL
H_C = 8
CHUNK = 128
D_FF = 2816
FFN_K = 3
FFN_BUF = FFN_K - 1
EPS = 1e-6

kernel_name = "pool_conv_sgu_hybrid_step"


def rmsnorm(x, g):
    xf = x.astype(jnp.float32)
    y = xf * lax.rsqrt(jnp.mean(xf * xf, axis=-1, keepdims=True) + EPS)
    return (y * g.astype(jnp.float32)).astype(x.dtype)


def layernorm(x, g, b):
    xf = x.astype(jnp.float32)
    mu = jnp.mean(xf, axis=-1, keepdims=True)
    var = jnp.mean(jnp.square(xf - mu), axis=-1, keepdims=True)
    y = (xf - mu) * lax.rsqrt(var + EPS) * g.astype(jnp.float32) + b.astype(jnp.float32)
    return y.astype(x.dtype)


def causal_dwconv(h, buf, w, b):
    k = w.shape[0]
    full = jnp.concatenate([buf.astype(h.dtype), h], axis=1)
    out = lax.conv_general_dilated(
        full, w.astype(h.dtype)[:, None, :], window_strides=(1,), padding="VALID",
        dimension_numbers=("NWC", "WIO", "NWC"), feature_group_count=h.shape[-1])
    return out + b.astype(h.dtype), full[:, full.shape[1] - (k - 1):]


def multi_scale_pool(xa, buf, start_pos, pool_w, pool_scale):
    bsz, t = xa.shape[0], xa.shape[1]
    full = jnp.concatenate([buf.astype(xa.dtype), xa], axis=1)
    cs = jnp.cumsum(full.astype(jnp.float32), axis=1)
    cs = jnp.pad(cs, ((0, 0), (1, 0), (0, 0)))
    pos = start_pos + jnp.arange(t, dtype=jnp.int32)
    means = []
    for g, w in enumerate(POOL_WINDOWS):
        sl = slice(g * POOL_GROUP, (g + 1) * POOL_GROUP)
        win = cs[:, POOL_BUF + 1:POOL_BUF + 1 + t, sl] - cs[:, POOL_BUF + 1 - w:POOL_BUF + 1 - w + t, sl]
        cnt = jnp.minimum(pos + 1, w).astype(jnp.float32)
        means.append(win / cnt[None, :, None])
    d = (jnp.concatenate(means, axis=-1) - xa.astype(jnp.float32)).astype(xa.dtype)
    d = d.reshape(bsz, t, N_POOL_GROUPS, POOL_GROUP)
    y = jnp.einsum("btgi,gio->btgo", d, pool_w).reshape(bsz, t, D_A) * pool_scale
    return y, full[:, full.shape[1] - POOL_BUF:]


def spatial_gating(v, w_s, b_s):
    bsz, t = v.shape[0], v.shape[1]
    ln = min(t, CHUNK)
    n = -(-t // ln)
    vp = jnp.pad(v, ((0, 0), (0, n * ln - t), (0, 0)))
    vc = vp.reshape(bsz, n, ln, H_C, D_C // H_C)
    w = jnp.tril(w_s[:, :ln, :ln])
    z = jnp.einsum("hij,bnjhd->bnihd", w, vc) + jnp.transpose(b_s[:, :ln])[None, None, :, :, None]
    return z.reshape(bsz, n * ln, D_C)[:, :t]


def trunk(x, start_pos, pool_bufs, conv_bufs, ffn_bufs, p):
    t = x.shape[1]
    new_pool, new_conv, new_v, new_ffn = [], [], [], []
    for layer in range(DEPTH):
        h = rmsnorm(x, p["mix_norm"][layer])
        if layer % 2 == 0:
            i = layer // 2
            proj = h @ p["w_in_ab"][i]
            xa = proj[..., :D_A]
            a = proj[..., D_A:D_A + D_B]
            gt = proj[..., D_A + D_B:]
            ya, pb = multi_scale_pool(xa, pool_bufs[i], start_pos, p["pool_w"][i], p["pool_scale"][i])
            glu = a * jax.nn.sigmoid(gt)
            c, cb = causal_dwconv(glu, conv_bufs[i], p["conv_w"][i], p["conv_b"][i])
            yb = jax.nn.silu(layernorm(c, p["conv_norm_g"][i], p["conv_norm_b"][i]))
            x = x + jnp.concatenate([ya, yb], axis=-1) @ p["w_out_ab"][i]
            new_pool.append(pb)
            new_conv.append(cb)
        else:
            j = layer // 2
            proj = h @ p["w_in_c"][j]
            u = proj[..., :D_C]
            v = layernorm(proj[..., D_C:], p["sgu_norm_g"][j], p["sgu_norm_b"][j])
            z = spatial_gating(v, p["sgu_w"][j], p["sgu_bias"][j])
            x = x + (u * z) @ p["w_out_c"][j]
            last_start = ((t - 1) // CHUNK) * CHUNK
            new_v.append(v[:, last_start:])
        h = rmsnorm(x, p["ffn_norm"][layer])
        up = h @ p["w_up"][layer]
        c, fb = causal_dwconv(up, ffn_bufs[layer], p["ffn_conv_w"][layer], p["ffn_conv_b"][layer])
        x = x + (jax.nn.silu(c[..., :D_FF]) * c[..., D_FF:]) @ p["w_down"][layer]
        new_ffn.append(fb)
    y = rmsnorm(x, p["final_norm"])
    return y, jnp.stack(new_pool), jnp.stack(new_conv), jnp.stack(new_v), jnp.stack(new_ffn)


def setup_inputs(seed: int = 0) -> dict:
    key = jax.random.key(seed)
    ks = jax.random.split(key, 32)
    f32 = jnp.float32

    def nrm(k, shape, scale):
        return jax.random.normal(k, shape, f32) * scale

    return {
        "x_prompt": nrm(ks[0], (BATCH, SEQ, D_MODEL), 1.0),
        "x_sample": nrm(ks[1], (DEC_BATCH, DEC_SEQ, D_MODEL), 1.0),
        "state_pool": nrm(ks[2], (N_AB, DEC_BATCH, POOL_BUF, D_A), 1.0),
        "state_conv": nrm(ks[3], (N_AB, DEC_BATCH, CONV_BUF, D_B), 0.5),
        "state_ffn": nrm(ks[4], (DEPTH, DEC_BATCH, FFN_BUF, 2 * D_FF), 1.0),
        "mix_norm": 1.0 + nrm(ks[5], (DEPTH, D_MODEL), 0.02),
        "w_in_ab": nrm(ks[6], (N_AB, D_MODEL, D_A + 2 * D_B), D_MODEL ** -0.5),
        "pool_w": nrm(ks[7], (N_AB, N_POOL_GROUPS, POOL_GROUP, POOL_GROUP), POOL_GROUP ** -0.5),
        "pool_scale": 0.5 + nrm(ks[8], (N_AB, D_A), 0.05),
        "conv_w": nrm(ks[9], (N_AB, CONV_K, D_B), CONV_K ** -0.5),
        "conv_b": nrm(ks[10], (N_AB, D_B), 0.02),
        "conv_norm_g": 1.0 + nrm(ks[11], (N_AB, D_B), 0.02),
        "conv_norm_b": nrm(ks[12], (N_AB, D_B), 0.02),
        "w_out_ab": nrm(ks[13], (N_AB, D_A + D_B, D_MODEL), (D_A + D_B) ** -0.5),
        "w_in_c": nrm(ks[14], (N_C, D_MODEL, 2 * D_C), D_MODEL ** -0.5),
        "sgu_norm_g": 1.0 + nrm(ks[15], (N_C, D_C), 0.02),
        "sgu_norm_b": nrm(ks[16], (N_C, D_C), 0.02),
        "sgu_w": nrm(ks[17], (N_C, H_C, CHUNK, CHUNK), 0.5 * CHUNK ** -0.5),
        "sgu_bias": 1.0 + nrm(ks[18], (N_C, H_C, CHUNK), 0.1),
        "w_out_c": nrm(ks[19], (N_C, D_C, D_MODEL), D_C ** -0.5),
        "ffn_norm": 1.0 + nrm(ks[20], (DEPTH, D_MODEL), 0.02),
        "w_up": nrm(ks[21], (DEPTH, D_MODEL, 2 * D_FF), D_MODEL ** -0.5),
        "ffn_conv_w": nrm(ks[22], (DEPTH, FFN_K, 2 * D_FF), FFN_K ** -0.5),
        "ffn_conv_b": nrm(ks[23], (DEPTH, 2 * D_FF), 0.02),
        "w_down": nrm(ks[24], (DEPTH, D_FF, D_MODEL), D_FF ** -0.5),
        "final_norm": 1.0 + nrm(ks[25], (D_MODEL,), 0.02),
    }


def reference(x_prompt, x_sample, state_pool, state_conv, state_ffn, mix_norm, w_in_ab, pool_w,
              pool_scale, conv_w, conv_b, conv_norm_g, conv_norm_b, w_out_ab, w_in_c, sgu_norm_g,
              sgu_norm_b, sgu_w, sgu_bias, w_out_c, ffn_norm, w_up, ffn_conv_w, ffn_conv_b, w_down,
              final_norm):
    p = dict(mix_norm=mix_norm, w_in_ab=w_in_ab, pool_w=pool_w, pool_scale=pool_scale,
             conv_w=conv_w, conv_b=conv_b, conv_norm_g=conv_norm_g, conv_norm_b=conv_norm_b,
             w_out_ab=w_out_ab, w_in_c=w_in_c, sgu_norm_g=sgu_norm_g, sgu_norm_b=sgu_norm_b,
             sgu_w=sgu_w, sgu_bias=sgu_bias, w_out_c=w_out_c, ffn_norm=ffn_norm, w_up=w_up,
             ffn_conv_w=ffn_conv_w, ffn_conv_b=ffn_conv_b, w_down=w_down, final_norm=final_norm)
    bp = x_prompt.shape[0]
    dt = x_prompt.dtype
    zero_pool = jnp.zeros((N_AB, bp, POOL_BUF, D_A), dt)
    zero_conv = jnp.zeros((N_AB, bp, CONV_BUF, D_B), dt)
    zero_ffn = jnp.zeros((DEPTH, bp, FFN_BUF, 2 * D_FF), dt)
    y_prompt, pool_prompt, conv_prompt, sgu_v_prompt, ffn_prompt = trunk(
        x_prompt, 0, zero_pool, zero_conv, zero_ffn, p)
    y_sample, pool_sample, conv_sample, sgu_v_sample, ffn_sample = trunk(
        x_sample, PAST_LEN, state_pool, state_conv, state_ffn, p)
    return (y_prompt, y_sample, pool_prompt, pool_sample, conv_prompt, conv_sample,
            sgu_v_prompt, sgu_v_sample, ffn_prompt, ffn_sample)
```

```python
import functools

import jax
import jax.numpy as jnp
from jax import lax
from jax.experimental import pallas as pl
from jax.experimental.pallas import tpu as pltpu

F32 = jnp.float32
BF16 = jnp.bfloat16

EPS = 1e-6
POOL_WINDOWS = (2, 4, 8, 16)
POOL_BUF = max(POOL_WINDOWS) - 1
CHUNK = 128
PAST_LEN = 16384
LANES = 128
FF_CHUNK = 256
CONV_ROWS = 32
VMEM_LIMIT = 56 * 1024 * 1024


def _rmsnorm(x, g):
    ms = jnp.mean(x * x, axis=-1, keepdims=True)
    return x * lax.rsqrt(ms + EPS) * g


def _layernorm(x, g, b):
    mu = jnp.mean(x, axis=-1, keepdims=True)
    xc = x - mu
    var = jnp.mean(xc * xc, axis=-1, keepdims=True)
    return xc * lax.rsqrt(var + EPS) * g + b


def _silu(x):
    return x * jax.nn.sigmoid(x)


def _bdot(a, b):
    return jnp.dot(a, b, preferred_element_type=F32)


def _mixer_ab_kernel(tt, bb, has_state, start_pos, *refs):
    if has_state:
        x_ref, sp_ref, sc_ref = refs[:3]
        refs = refs[3:]
    else:
        x_ref = refs[0]
        refs = refs[1:]
    (g_ref, win_ref, pw_ref, ps_ref, cw_ref, cb_ref, lg_ref, lb_ref, wout_ref,
     xo_ref, po_ref, co_ref, xa_full, glu_full, yb_ref) = refs
    rows = tt * bb
    d_a = xa_full.shape[1]
    d_b = glu_full.shape[1]
    conv_k = cw_ref.shape[0]
    hp = POOL_BUF * bb
    hc = (conv_k - 1) * bb
    ti = pl.program_id(1)

    if has_state:
        xa_full[0:hp, :] = sp_ref[...].reshape(hp, d_a)
        glu_full[0:hc, :] = sc_ref[...].reshape(hc, d_b)
    else:
        @pl.when(ti == 0)
        def _():
            xa_full[0:hp, :] = jnp.zeros((hp, d_a), F32)
            glu_full[0:hc, :] = jnp.zeros((hc, d_b), F32)

        @pl.when(ti > 0)
        def _():
            xa_full[0:hp, :] = xa_full[rows:rows + hp, :]
            glu_full[0:hc, :] = glu_full[rows:rows + hc, :]

    x = x_ref[...].reshape(rows, x_ref.shape[2])
    h = _rmsnorm(x, g_ref[...]).astype(BF16)
    proj = _bdot(h, win_ref[...])
    xa = proj[:, :d_a]
    glu = proj[:, d_a:d_a + d_b] * jax.nn.sigmoid(proj[:, d_a + d_b:])
    xa_full[hp:hp + rows, :] = xa
    glu_full[hc:hc + rows, :] = glu
    po_ref[...] = xa_full[rows:rows + hp, :].reshape(po_ref.shape)
    co_ref[...] = glu_full[rows:rows + hc, :].reshape(co_ref.shape)

    pos = start_pos + ti * tt + lax.broadcasted_iota(jnp.int32, (rows, LANES), 0) // bb
    group = d_a // len(POOL_WINDOWS)
    out = x
    for gi, w in enumerate(POOL_WINDOWS):
        c0 = gi * group
        xg = xa[:, c0:c0 + group]
        win = xg
        for k in range(1, w):
            win = win + xa_full[hp - k * bb:hp - k * bb + rows, c0:c0 + group]
        cnt = jnp.minimum(pos + 1, w).astype(F32)
        d = (win / cnt - xg).astype(BF16)
        ya = _bdot(d, pw_ref[gi]) * ps_ref[:, c0:c0 + group]
        out = out + _bdot(ya.astype(BF16), wout_ref[c0:c0 + group, :])

    def conv_block(r, carry):
        r0 = pl.multiple_of(r * CONV_ROWS, CONV_ROWS)
        acc = jnp.broadcast_to(cb_ref[...], (CONV_ROWS, d_b))
        for k in range(conv_k):
            acc = acc + glu_full[pl.ds(r0 + k * bb, CONV_ROWS), :] * cw_ref[k:k + 1, :]
        yb = _silu(_layernorm(acc, lg_ref[...], lb_ref[...]))
        yb_ref[pl.ds(r0, CONV_ROWS), :] = yb.astype(BF16)
        return carry

    lax.fori_loop(0, rows // CONV_ROWS, conv_block, 0)
    out = out + _bdot(yb_ref[...], wout_ref[d_a:d_a + d_b, :])
    xo_ref[...] = out.reshape(xo_ref.shape)


def _ffn_kernel(tt, bb, has_state, final_norm, *refs):
    if has_state:
        x_ref, st_ref = refs[:2]
        refs = refs[2:]
    else:
        x_ref = refs[0]
        refs = refs[1:]
    (g_ref, wup_ref, fw_ref, fb_ref, wdown_ref, fin_ref,
     xo_ref, so_ref, h_ref, acc_ref, full_a_ref, full_g_ref, carry_ref) = refs
    rows = tt * bb
    d_ff = wdown_ref.shape[0]
    halo = (fw_ref.shape[0] - 1) * bb
    ti = pl.program_id(1)

    x = x_ref[...].reshape(rows, x_ref.shape[2])
    h_ref[...] = _rmsnorm(x, g_ref[...]).astype(BF16)
    acc_ref[...] = x

    if not has_state:
        @pl.when(ti == 0)
        def _():
            carry_ref[...] = jnp.zeros(carry_ref.shape, F32)

    for c in range(d_ff // FF_CHUNK):
        conv = []
        for full_ref, off in ((full_a_ref, c * FF_CHUNK), (full_g_ref, d_ff + c * FF_CHUNK)):
            up = _bdot(h_ref[...], wup_ref[:, off:off + FF_CHUNK])
            if has_state:
                full_ref[0:halo, :] = st_ref[:, :, off:off + FF_CHUNK].reshape(halo, FF_CHUNK)
            else:
                full_ref[0:halo, :] = carry_ref[:, off:off + FF_CHUNK]
            full_ref[halo:halo + rows, :] = up
            tail = full_ref[rows:rows + halo, :]
            so_ref[:, :, off:off + FF_CHUNK] = tail.reshape(so_ref.shape[0], bb, FF_CHUNK)
            if not has_state:
                carry_ref[:, off:off + FF_CHUNK] = tail
            cv = (full_ref[0:rows, :] * fw_ref[0:1, off:off + FF_CHUNK]
                  + full_ref[bb:bb + rows, :] * fw_ref[1:2, off:off + FF_CHUNK]
                  + up * fw_ref[2:3, off:off + FF_CHUNK]
                  + fb_ref[:, off:off + FF_CHUNK])
            conv.append(cv)
        act = (_silu(conv[0]) * conv[1]).astype(BF16)
        acc_ref[...] += _bdot(act, wdown_ref[c * FF_CHUNK:(c + 1) * FF_CHUNK, :])

    out = acc_ref[...]
    if final_norm:
        out = _rmsnorm(out, fin_ref[...])
    xo_ref[...] = out.reshape(xo_ref.shape)


def _mixer_c_kernel(tt, bb, chunked, *refs):
    (x_ref, g_ref, win_ref, lg_ref, lb_ref, sw_ref, sb_ref, wout_ref,
     xo_ref, vo_ref, xs_ref, vb_ref, z_ref) = refs
    rows = tt * bb
    d_model = x_ref.shape[2]
    d_c = wout_ref.shape[0]

    if chunked:
        for b in range(bb):
            xs_ref[b * tt:(b + 1) * tt, :] = x_ref[:, b, :]
        x = xs_ref[...]
    else:
        x = x_ref[...].reshape(rows, d_model)
    h = _rmsnorm(x, g_ref[...]).astype(BF16)
    proj = _bdot(h, win_ref[...])
    u = proj[:, :d_c]
    v = _layernorm(proj[:, d_c:], lg_ref[...], lb_ref[...])
    vo_ref[...] = v.reshape(vo_ref.shape)

    if chunked:
        n_heads = sw_ref.shape[0]
        dh = d_c // n_heads
        vb_ref[...] = v.astype(BF16)
        row_i = lax.broadcasted_iota(jnp.int32, (tt, tt), 0)
        col_i = lax.broadcasted_iota(jnp.int32, (tt, tt), 1)
        for hd in range(n_heads):
            m = jnp.where(col_i <= row_i, sw_ref[hd], 0.0).astype(BF16)
            rhs = jnp.concatenate(
                [vb_ref[b * tt:(b + 1) * tt, hd * dh:(hd + 1) * dh] for b in range(bb)], axis=1)
            zc = _bdot(m, rhs)
            bias = sb_ref[:, hd * dh:(hd + 1) * dh]
            for b in range(bb):
                z_ref[b * tt:(b + 1) * tt, hd * dh:(hd + 1) * dh] = zc[:, b * dh:(b + 1) * dh] + bias
    else:
        xs_ref[...] = v
        for i in range(tt):
            zi = jnp.broadcast_to(sb_ref[i:i + 1, :], (bb, d_c))
            for j in range(i + 1):
                zi = zi + xs_ref[j * bb:(j + 1) * bb, :] * sw_ref[i * tt + j:i * tt + j + 1, :]
            z_ref[i * bb:(i + 1) * bb, :] = zi

    gated = (u * z_ref[...]).astype(BF16)
    out = x + _bdot(gated, wout_ref[...])
    if chunked:
        for b in range(bb):
            xo_ref[:, b, :] = out[b * tt:(b + 1) * tt, :]
    else:
        xo_ref[...] = out.reshape(xo_ref.shape)


def _const_spec(shape):
    zeros = (0,) * len(shape)
    return pl.BlockSpec(shape, lambda bi, ti: zeros, pipeline_mode=pl.Buffered(1))


def _tile_spec(tt, bb, width):
    return pl.BlockSpec((tt, bb, width), lambda bi, ti: (ti, bi, 0))


def _state_spec(n, bb, width):
    return pl.BlockSpec((n, bb, width), lambda bi, ti: (0, bi, 0))


_PARAMS = pltpu.CompilerParams(
    dimension_semantics=("arbitrary", "arbitrary"), vmem_limit_bytes=VMEM_LIMIT)


def _mixer_ab(x, states, w, tt, bb, start_pos, name):
    t_len, batch, d_model = x.shape
    d_a = w["pool_scale"].shape[1]
    d_b = w["conv_b"].shape[1]
    conv_k = w["conv_w"].shape[0]
    rows = tt * bb
    has_state = states is not None
    weights = (w["mix_norm"], w["w_in_ab"], w["pool_w"], w["pool_scale"], w["conv_w"], w["conv_b"],
               w["conv_norm_g"], w["conv_norm_b"], w["w_out_ab"])
    args = (x,) + (tuple(states) if has_state else ()) + weights
    in_specs = [_tile_spec(tt, bb, d_model)]
    if has_state:
        in_specs += [_state_spec(POOL_BUF, bb, d_a), _state_spec(conv_k - 1, bb, d_b)]
    in_specs += [_const_spec(a.shape) for a in weights]
    return pl.pallas_call(
        functools.partial(_mixer_ab_kernel, tt, bb, has_state, start_pos),
        grid=(batch // bb, t_len // tt),
        in_specs=in_specs,
        out_specs=[_tile_spec(tt, bb, d_model), _state_spec(POOL_BUF, bb, d_a),
                   _state_spec(conv_k - 1, bb, d_b)],
        out_shape=[jax.ShapeDtypeStruct(x.shape, F32),
                   jax.ShapeDtypeStruct((POOL_BUF, batch, d_a), F32),
                   jax.ShapeDtypeStruct((conv_k - 1, batch, d_b), F32)],
        scratch_shapes=[pltpu.VMEM(((tt + POOL_BUF) * bb, d_a), F32),
                        pltpu.VMEM(((tt + conv_k - 1) * bb, d_b), F32),
                        pltpu.VMEM((rows, d_b), BF16)],
        compiler_params=_PARAMS,
        name=name,
    )(*args)


def _ffn(x, state, w, tt, bb, final_norm, name):
    t_len, batch, d_model = x.shape
    d_ff = w["w_down"].shape[0]
    ffn_k = w["ffn_conv_w"].shape[0]
    rows = tt * bb
    halo = (ffn_k - 1) * bb
    has_state = state is not None
    weights = (w["ffn_norm"], w["w_up"], w["ffn_conv_w"], w["ffn_conv_b"], w["w_down"], w["final_norm"])
    args = (x,) + ((state,) if has_state else ()) + weights
    in_specs = [_tile_spec(tt, bb, d_model)]
    if has_state:
        in_specs += [_state_spec(ffn_k - 1, bb, 2 * d_ff)]
    in_specs += [_const_spec(a.shape) for a in weights]
    return pl.pallas_call(
        functools.partial(_ffn_kernel, tt, bb, has_state, final_norm),
        grid=(batch // bb, t_len // tt),
        in_specs=in_specs,
        out_specs=[_tile_spec(tt, bb, d_model), _state_spec(ffn_k - 1, bb, 2 * d_ff)],
        out_shape=[jax.ShapeDtypeStruct(x.shape, F32),
                   jax.ShapeDtypeStruct((ffn_k - 1, batch, 2 * d_ff), F32)],
        scratch_shapes=[pltpu.VMEM((rows, d_model), BF16),
                        pltpu.VMEM((rows, d_model), F32),
                        pltpu.VMEM((rows + halo, FF_CHUNK), F32),
                        pltpu.VMEM((rows + halo, FF_CHUNK), F32),
                        pltpu.VMEM((halo, 2 * d_ff), F32)],
        compiler_params=_PARAMS,
        name=name,
    )(*args)


def _mixer_c(x, w, tt, bb, chunked, name):
    t_len, batch, d_model = x.shape
    d_c = w["w_out_c"].shape[0]
    rows = tt * bb
    weights = (w["mix_norm"], w["w_in_c"], w["sgu_norm_g"], w["sgu_norm_b"],
               w["sgu_w_c"] if chunked else w["sgu_w_s"],
               w["sgu_b_c"] if chunked else w["sgu_b_s"], w["w_out_c"])
    if chunked:
        v_spec = pl.BlockSpec((bb, tt, d_c), lambda bi, ti: (bi, 0, 0))
        v_shape = jax.ShapeDtypeStruct((batch, tt, d_c), F32)
    else:
        v_spec = _tile_spec(tt, bb, d_c)
        v_shape = jax.ShapeDtypeStruct((t_len, batch, d_c), F32)
    return pl.pallas_call(
        functools.partial(_mixer_c_kernel, tt, bb, chunked),
        grid=(batch // bb, t_len // tt),
        in_specs=[_tile_spec(tt, bb, d_model)] + [_const_spec(a.shape) for a in weights],
        out_specs=[_tile_spec(tt, bb, d_model), v_spec],
        out_shape=[jax.ShapeDtypeStruct(x.shape, F32), v_shape],
        scratch_shapes=[pltpu.VMEM((rows, d_model), F32),
                        pltpu.VMEM((rows, d_c), BF16),
                        pltpu.VMEM((rows, d_c), F32)],
        compiler_params=_PARAMS,
        name=name,
    )(x, *weights)


def _trunk(x, states, layers, cfg, start_pos, tag):
    l0, l1 = layers
    has_state = states is not None
    x, pool, conv = _mixer_ab(x, (states[0], states[1]) if has_state else None, l0,
                              cfg["tt"], cfg["bb"], start_pos, "mixer_ab_" + tag)
    x, ffn0 = _ffn(x, states[2] if has_state else None, l0, cfg["tt"], cfg["bb"], False, "ffn0_" + tag)
    x, v = _mixer_c(x, l1, cfg["tt_c"], cfg["bb"], cfg["chunked"], "mixer_c_" + tag)
    x, ffn1 = _ffn(x, states[3] if has_state else None, l1, cfg["tt"], cfg["bb"], True, "ffn1_" + tag)
    return x, pool, conv, v, ffn0, ffn1


def _tm(a):
    return jnp.swapaxes(a, 0, 1)


def kernel(x_prompt, x_sample, state_pool, state_conv, state_ffn, mix_norm, w_in_ab, pool_w, pool_scale, conv_w, conv_b, conv_norm_g, conv_norm_b, w_out_ab, w_in_c, sgu_norm_g, sgu_norm_b, sgu_w, sgu_bias, w_out_c, ffn_norm, w_up, ffn_conv_w, ffn_conv_b, w_down, final_norm):
    dec_seq = x_sample.shape[1]
    n_heads = sgu_w.shape[1]
    d_c = w_out_c.shape[1]
    dh = d_c // n_heads
    row = lambda a: a.reshape(1, -1)
    fin = row(final_norm)
    l0 = dict(mix_norm=row(mix_norm[0]), w_in_ab=w_in_ab[0].astype(BF16), pool_w=pool_w[0].astype(BF16),
              pool_scale=row(pool_scale[0]), conv_w=conv_w[0], conv_b=row(conv_b[0]),
              conv_norm_g=row(conv_norm_g[0]), conv_norm_b=row(conv_norm_b[0]),
              w_out_ab=w_out_ab[0].astype(BF16),
              ffn_norm=row(ffn_norm[0]), w_up=w_up[0].astype(BF16), ffn_conv_w=ffn_conv_w[0],
              ffn_conv_b=row(ffn_conv_b[0]), w_down=w_down[0].astype(BF16), final_norm=fin)
    sw_short = jnp.repeat(jnp.transpose(sgu_w[0][:, :dec_seq, :dec_seq], (1, 2, 0)), dh, axis=2)
    l1 = dict(mix_norm=row(mix_norm[1]), w_in_c=w_in_c[0].astype(BF16),
              sgu_norm_g=row(sgu_norm_g[0]), sgu_norm_b=row(sgu_norm_b[0]),
              sgu_w_c=sgu_w[0], sgu_b_c=jnp.repeat(sgu_bias[0].T, dh, axis=1),
              sgu_w_s=sw_short.reshape(dec_seq * dec_seq, d_c),
              sgu_b_s=jnp.repeat(sgu_bias[0][:, :dec_seq].T, dh, axis=1),
              w_out_c=w_out_c[0].astype(BF16),
              ffn_norm=row(ffn_norm[1]), w_up=w_up[1].astype(BF16), ffn_conv_w=ffn_conv_w[1],
              ffn_conv_b=row(ffn_conv_b[1]), w_down=w_down[1].astype(BF16), final_norm=fin)
    layers = (l0, l1)

    cfg_p = dict(tt=64, bb=x_prompt.shape[0], tt_c=CHUNK, chunked=True)
    yp, pool_p, conv_p, v_p, f0_p, f1_p = _trunk(_tm(x_prompt), None, layers, cfg_p, 0, "prompt")

    cfg_s = dict(tt=dec_seq, bb=64, tt_c=dec_seq, chunked=False)
    states = (_tm(state_pool[0]), _tm(state_conv[0]), _tm(state_ffn[0]), _tm(state_ffn[1]))
    ys, pool_s, conv_s, v_s, f0_s, f1_s = _trunk(_tm(x_sample), states, layers, cfg_s, PAST_LEN, "sample")

    return (_tm(yp), _tm(ys), _tm(pool_p)[None], _tm(pool_s)[None], _tm(conv_p)[None], _tm(conv_s)[None],
            v_p[None], _tm(v_s)[None],
            jnp.stack([_tm(f0_p), _tm(f1_p)]), jnp.stack([_tm(f0_s), _tm(f1_s)]))
```

```python
import functools

import jax
import jax.numpy as jnp
from jax import lax
from jax.experimental import pallas as pl
from jax.experimental.pallas import tpu as pltpu

F32 = jnp.float32
BF16 = jnp.bfloat16

EPS = 1e-6
POOL_WINDOWS = (2, 4, 8, 16)
POOL_BUF = max(POOL_WINDOWS) - 1
CHUNK = 128
PAST_LEN = 16384
LANES = 128
SUBLANES = 8
FF_CHUNK = 256
DOWN_GROUP = 2
CONV_ROWS = 32
VMEM_LIMIT = 56 * 1024 * 1024


def _rmsnorm(x, g):
    ms = jnp.mean(x * x, axis=-1, keepdims=True)
    return x * lax.rsqrt(ms + EPS) * g


def _layernorm(x, g, b):
    mu = jnp.mean(x, axis=-1, keepdims=True)
    xc = x - mu
    var = jnp.mean(xc * xc, axis=-1, keepdims=True)
    return xc * lax.rsqrt(var + EPS) * g + b


def _silu(x):
    return x * jax.nn.sigmoid(x)


def _bdot(a, b):
    return jnp.dot(a, b, preferred_element_type=F32)


def _mixer_ab_kernel(tt, bb, has_state, permute, start_pos, *refs):
    x_ref, refs = refs[0], refs[1:]
    if has_state:
        sp_ref, sc_ref = refs[:2]
        refs = refs[2:]
    if permute:
        p_ref, pt_ref = refs[:2]
        refs = refs[2:]
    (g_ref, win_ref, pw_ref, ps_ref, cw_ref, cb_ref, lg_ref, lb_ref, wout_ref,
     xo_ref, po_ref, co_ref, xa_full, glu_full, c_ref) = refs
    rows = tt * bb
    d_a = xa_full.shape[1]
    d_b = glu_full.shape[1]
    conv_k = cw_ref.shape[0]
    hp = POOL_BUF * bb
    hc = (conv_k - 1) * bb
    ti = pl.program_id(1)

    if has_state:
        xa_full[0:hp, :] = sp_ref[...].reshape(hp, d_a)
        glu_full[0:hc, :] = sc_ref[...].reshape(hc, d_b)
    else:
        @pl.when(ti == 0)
        def _():
            xa_full[0:hp, :] = jnp.zeros((hp, d_a), F32)
            glu_full[0:hc, :] = jnp.zeros((hc, d_b), F32)

        @pl.when(ti > 0)
        def _():
            xa_full[0:hp, :] = xa_full[rows:rows + hp, :]
            glu_full[0:hc, :] = glu_full[rows:rows + hc, :]

    x = x_ref[...].reshape(rows, x_ref.shape[2])
    h = _rmsnorm(x, g_ref[...]).astype(BF16)
    if permute:
        h = _bdot(p_ref[...], h).astype(BF16)
    proj = _bdot(h, win_ref[...])
    xa = proj[:, :d_a]
    glu = proj[:, d_a:d_a + d_b] * jax.nn.sigmoid(proj[:, d_a + d_b:])
    xa_full[hp:hp + rows, :] = xa
    glu_full[hc:hc + rows, :] = glu
    po_ref[...] = xa_full[rows:rows + hp, :].reshape(po_ref.shape)
    co_ref[...] = glu_full[rows:rows + hc, :].reshape(co_ref.shape)

    pos = start_pos + ti * tt + lax.broadcasted_iota(jnp.int32, (rows, LANES), 0) // bb
    group = d_a // len(POOL_WINDOWS)
    ya = []
    for gi, w in enumerate(POOL_WINDOWS):
        c0 = gi * group
        xg = xa[:, c0:c0 + group]
        win = xg
        for k in range(1, w):
            win = win + xa_full[hp - k * bb:hp - k * bb + rows, c0:c0 + group]
        cnt = jnp.minimum(pos + 1, w).astype(F32)
        d = (win / cnt - xg).astype(BF16)
        ya.append((_bdot(d, pw_ref[gi]) * ps_ref[:, c0:c0 + group]).astype(BF16))

    def conv_block(r, carry):
        r0 = pl.multiple_of(r * CONV_ROWS, CONV_ROWS)
        acc = jnp.broadcast_to(cb_ref[...], (CONV_ROWS, d_b))
        for k in range(conv_k):
            acc = acc + glu_full[pl.ds(r0 + k * bb, CONV_ROWS), :] * cw_ref[k:k + 1, :]
        c_ref[pl.ds(r0, CONV_ROWS), :] = acc
        return carry

    lax.fori_loop(0, rows // CONV_ROWS, conv_block, 0)
    yb = _silu(_layernorm(c_ref[...], lg_ref[...], lb_ref[...])).astype(BF16)
    y = jnp.concatenate(ya + [yb], axis=1)
    if permute:
        y = _bdot(pt_ref[...], y).astype(BF16)
    out = x + _bdot(y, wout_ref[...])
    xo_ref[...] = out.reshape(xo_ref.shape)


def _ffn_conv_slabs(full_ref, up, halo_rows, fw, fb, bb, rows):
    halo = halo_rows.shape[0]
    full_ref[0:halo, :] = halo_rows
    full_ref[halo:halo + rows, :] = up
    tail = full_ref[rows:rows + halo, :]
    cv = (full_ref[0:rows, :] * fw[0:1, :] + full_ref[bb:bb + rows, :] * fw[1:2, :]
          + up * fw[2:3, :] + fb)
    return cv, tail


def _ffn_conv_rolled(up, prev8, fw, fb):
    rows, width = up.shape
    groups = up.reshape(rows // SUBLANES, SUBLANES, width)
    ext = jnp.concatenate([prev8[None], groups], axis=0)
    sub = lax.broadcasted_iota(jnp.int32, (1, SUBLANES, width), 1)
    cv = groups * fw[2:3, :] + fb
    for shift in (1, 2):
        rot = pltpu.roll(ext, shift, 1)
        cv = cv + jnp.where(sub < shift, rot[:-1], rot[1:]) * fw[2 - shift:3 - shift, :]
    return cv.reshape(rows, width)


def _ffn_kernel(tt, bb, has_state, final_norm, *refs):
    x_ref, refs = refs[0], refs[1:]
    if has_state:
        st_ref, refs = refs[0], refs[1:]
    (g_ref, wup_ref, fw_ref, fb_ref, wdown_ref, fin_ref,
     xo_ref, so_ref, h_ref, acc_ref, full_a_ref, full_g_ref, carry_ref) = refs
    rows = tt * bb
    d_ff = wdown_ref.shape[0]
    n_keep = fw_ref.shape[0] - 1
    ti = pl.program_id(1)

    x = x_ref[...].reshape(rows, x_ref.shape[2])
    h_ref[...] = _rmsnorm(x, g_ref[...]).astype(BF16)
    acc_ref[...] = x

    if not has_state:
        @pl.when(ti == 0)
        def _():
            carry_ref[...] = jnp.zeros(carry_ref.shape, F32)

    def up_pair(c):
        return [_bdot(h_ref[...], wup_ref[:, off:off + FF_CHUNK])
                for off in (c * FF_CHUNK, d_ff + c * FF_CHUNK)]

    n_chunks = d_ff // FF_CHUNK
    ups_next = up_pair(0)
    acts = []
    for c in range(n_chunks):
        ups = ups_next
        if c + 1 < n_chunks:
            ups_next = up_pair(c + 1)
        conv = []
        for full_ref, off, up in ((full_a_ref, c * FF_CHUNK, ups[0]),
                                  (full_g_ref, d_ff + c * FF_CHUNK, ups[1])):
            cols = slice(off, off + FF_CHUNK)
            if bb == 1:
                cv = _ffn_conv_rolled(up, carry_ref[:, cols], fw_ref[:, cols], fb_ref[:, cols])
                carry_ref[:, cols] = up[rows - SUBLANES:, :]
                so_ref[0, :, cols] = up[rows - n_keep:, :]
            else:
                if has_state:
                    halo_rows = st_ref[:, :, cols].reshape(n_keep * bb, FF_CHUNK)
                else:
                    halo_rows = carry_ref[:, cols]
                cv, tail = _ffn_conv_slabs(full_ref, up, halo_rows, fw_ref[:, cols], fb_ref[:, cols],
                                           bb, rows)
                so_ref[:, :, cols] = tail.reshape(n_keep, bb, FF_CHUNK)
                if not has_state:
                    carry_ref[:, cols] = tail
            conv.append(cv)
        acts.append((_silu(conv[0]) * conv[1]).astype(BF16))
        if len(acts) == DOWN_GROUP or c + 1 == n_chunks:
            lo = (c + 1 - len(acts)) * FF_CHUNK
            act = acts[0] if len(acts) == 1 else jnp.concatenate(acts, axis=1)
            acc_ref[...] += _bdot(act, wdown_ref[lo:(c + 1) * FF_CHUNK, :])
            acts = []

    out = acc_ref[...]
    if final_norm:
        out = _rmsnorm(out, fin_ref[...])
    xo_ref[...] = out.reshape(xo_ref.shape)


def _mixer_c_kernel(tt, bb, *refs):
    (x_ref, g_ref, win_ref, lg_ref, lb_ref, sw_ref, sb_ref, wout_ref,
     xo_ref, vo_ref, v_ref, vb_ref, z_ref) = refs
    rows = tt * bb
    d_c = wout_ref.shape[0]

    x = x_ref[...].reshape(rows, x_ref.shape[2])
    h = _rmsnorm(x, g_ref[...]).astype(BF16)
    proj = _bdot(h, win_ref[...])
    u = proj[:, :d_c]
    v = _layernorm(proj[:, d_c:], lg_ref[...], lb_ref[...])

    if bb == 1:
        n_heads = sw_ref.shape[0]
        dh = d_c // n_heads
        n_chunks = tt // CHUNK
        vo_ref[0] = v[rows - CHUNK:, :]
        vb_ref[...] = v.astype(BF16)
        row_i = lax.broadcasted_iota(jnp.int32, (CHUNK, CHUNK), 0)
        col_i = lax.broadcasted_iota(jnp.int32, (CHUNK, CHUNK), 1)
        for hd in range(n_heads):
            hcols = slice(hd * dh, (hd + 1) * dh)
            m = jnp.where(col_i <= row_i, sw_ref[hd], 0.0).astype(BF16)
            rhs = jnp.concatenate(
                [vb_ref[c * CHUNK:(c + 1) * CHUNK, hcols] for c in range(n_chunks)], axis=1)
            zc = _bdot(m, rhs)
            bias = sb_ref[:, hcols]
            for c in range(n_chunks):
                z_ref[c * CHUNK:(c + 1) * CHUNK, hcols] = zc[:, c * dh:(c + 1) * dh] + bias
    else:
        vo_ref[...] = v.reshape(vo_ref.shape)
        v_ref[...] = v
        for i in range(tt):
            zi = jnp.broadcast_to(sb_ref[i:i + 1, :], (bb, d_c))
            for j in range(i + 1):
                zi = zi + v_ref[j * bb:(j + 1) * bb, :] * sw_ref[i * tt + j:i * tt + j + 1, :]
            z_ref[i * bb:(i + 1) * bb, :] = zi

    gated = (u * z_ref[...]).astype(BF16)
    out = x + _bdot(gated, wout_ref[...])
    xo_ref[...] = out.reshape(xo_ref.shape)


def _const_spec(shape):
    zeros = (0,) * len(shape)
    return pl.BlockSpec(shape, lambda bi, ti: zeros, pipeline_mode=pl.Buffered(1))


def _tm_spec(tt, bb, width):
    return pl.BlockSpec((tt, bb, width), lambda bi, ti: (ti, bi, 0))


def _bm_spec(bb, tt, width):
    return pl.BlockSpec((bb, tt, width), lambda bi, ti: (bi, ti, 0))


def _tm_state_spec(n, bb, width):
    return pl.BlockSpec((n, bb, width), lambda bi, ti: (0, bi, 0))


_PARAMS = pltpu.CompilerParams(
    dimension_semantics=("arbitrary", "arbitrary"), vmem_limit_bytes=VMEM_LIMIT)


def _row_permutation(tt, bb):
    m = jnp.arange(tt * bb)
    src = (m % bb) * tt + m // bb
    return (src[:, None] == jnp.arange(tt * bb)[None, :]).astype(BF16)


def _mixer_ab(x, states, w, tt, bb, permute, start_pos, name):
    if permute:
        batch, t_len, d_model = x.shape
        x_spec = _bm_spec(bb, tt, d_model)
    else:
        t_len, batch, d_model = x.shape
        x_spec = _tm_spec(tt, bb, d_model)
    d_a = w["pool_scale"].shape[1]
    d_b = w["conv_b"].shape[1]
    conv_k = w["conv_w"].shape[0]
    rows = tt * bb
    has_state = states is not None
    consts = ()
    if permute:
        perm = _row_permutation(tt, bb)
        consts = (perm, perm.T)
    consts += (w["mix_norm"], w["w_in_ab"], w["pool_w"], w["pool_scale"], w["conv_w"], w["conv_b"],
               w["conv_norm_g"], w["conv_norm_b"], w["w_out_ab"])
    in_specs = [x_spec]
    if has_state:
        in_specs += [_tm_state_spec(POOL_BUF, bb, d_a), _tm_state_spec(conv_k - 1, bb, d_b)]
    in_specs += [_const_spec(a.shape) for a in consts]
    return pl.pallas_call(
        functools.partial(_mixer_ab_kernel, tt, bb, has_state, permute, start_pos),
        grid=(batch // bb, t_len // tt),
        in_specs=in_specs,
        out_specs=[x_spec, _tm_state_spec(POOL_BUF, bb, d_a), _tm_state_spec(conv_k - 1, bb, d_b)],
        out_shape=[jax.ShapeDtypeStruct(x.shape, F32),
                   jax.ShapeDtypeStruct((POOL_BUF, batch, d_a), F32),
                   jax.ShapeDtypeStruct((conv_k - 1, batch, d_b), F32)],
        scratch_shapes=[pltpu.VMEM(((tt + POOL_BUF) * bb, d_a), F32),
                        pltpu.VMEM(((tt + conv_k - 1) * bb, d_b), F32),
                        pltpu.VMEM((rows, d_b), F32)],
        compiler_params=_PARAMS,
        name=name,
    )(x, *(tuple(states) if has_state else ()), *consts)


def _ffn(x, state, w, tt, bb, final_norm, name):
    d_ff = w["w_down"].shape[0]
    n_keep = w["ffn_conv_w"].shape[0] - 1
    has_state = state is not None
    if bb == 1:
        assert not has_state and tt % SUBLANES == 0
        batch, t_len, d_model = x.shape
        x_spec = _bm_spec(1, tt, d_model)
        st_spec = pl.BlockSpec((1, n_keep, 2 * d_ff), lambda bi, ti: (bi, 0, 0))
        st_shape = (batch, n_keep, 2 * d_ff)
        carry_rows = SUBLANES
    else:
        t_len, batch, d_model = x.shape
        x_spec = _tm_spec(tt, bb, d_model)
        st_spec = _tm_state_spec(n_keep, bb, 2 * d_ff)
        st_shape = (n_keep, batch, 2 * d_ff)
        carry_rows = n_keep * bb
    rows = tt * bb
    weights = (w["ffn_norm"], w["w_up"], w["ffn_conv_w"], w["ffn_conv_b"], w["w_down"], w["final_norm"])
    in_specs = [x_spec] + ([st_spec] if has_state else []) + [_const_spec(a.shape) for a in weights]
    return pl.pallas_call(
        functools.partial(_ffn_kernel, tt, bb, has_state, final_norm),
        grid=(batch // bb, t_len // tt),
        in_specs=in_specs,
        out_specs=[x_spec, st_spec],
        out_shape=[jax.ShapeDtypeStruct(x.shape, F32), jax.ShapeDtypeStruct(st_shape, F32)],
        scratch_shapes=[pltpu.VMEM((rows, d_model), BF16),
                        pltpu.VMEM((rows, d_model), F32),
                        pltpu.VMEM((rows + carry_rows, FF_CHUNK), F32),
                        pltpu.VMEM((rows + carry_rows, FF_CHUNK), F32),
                        pltpu.VMEM((carry_rows, 2 * d_ff), F32)],
        compiler_params=_PARAMS,
        name=name,
    )(x, *((state,) if has_state else ()), *weights)


def _mixer_c(x, w, tt, bb, name):
    d_c = w["w_out_c"].shape[0]
    if bb == 1:
        assert tt % CHUNK == 0
        batch, t_len, d_model = x.shape
        x_spec = _bm_spec(1, tt, d_model)
        v_spec = pl.BlockSpec((1, CHUNK, d_c), lambda bi, ti: (bi, 0, 0))
        v_shape = (batch, CHUNK, d_c)
        sgu = (w["sgu_w"], w["sgu_b_chunk"])
    else:
        t_len, batch, d_model = x.shape
        assert t_len == tt
        x_spec = _tm_spec(tt, bb, d_model)
        v_spec = _tm_spec(tt, bb, d_c)
        v_shape = (t_len, batch, d_c)
        sgu = (w["sgu_w_short"], w["sgu_b_short"])
    rows = tt * bb
    weights = (w["mix_norm"], w["w_in_c"], w["sgu_norm_g"], w["sgu_norm_b"]) + sgu + (w["w_out_c"],)
    return pl.pallas_call(
        functools.partial(_mixer_c_kernel, tt, bb),
        grid=(batch // bb, t_len // tt),
        in_specs=[x_spec] + [_const_spec(a.shape) for a in weights],
        out_specs=[x_spec, v_spec],
        out_shape=[jax.ShapeDtypeStruct(x.shape, F32), jax.ShapeDtypeStruct(v_shape, F32)],
        scratch_shapes=[pltpu.VMEM((rows, d_c), F32),
                        pltpu.VMEM((rows, d_c), BF16),
                        pltpu.VMEM((rows, d_c), F32)],
        compiler_params=_PARAMS,
        name=name,
    )(x, *weights)


def _tm(a):
    return jnp.swapaxes(a, 0, 1)


def kernel(x_prompt, x_sample, state_pool, state_conv, state_ffn, mix_norm, w_in_ab, pool_w, pool_scale, conv_w, conv_b, conv_norm_g, conv_norm_b, w_out_ab, w_in_c, sgu_norm_g, sgu_norm_b, sgu_w, sgu_bias, w_out_c, ffn_norm, w_up, ffn_conv_w, ffn_conv_b, w_down, final_norm):
    batch_p = x_prompt.shape[0]
    dec_seq = x_sample.shape[1]
    n_heads = sgu_w.shape[1]
    d_c = w_out_c.shape[1]
    dh = d_c // n_heads
    row = lambda a: a.reshape(1, -1)
    fin = row(final_norm)
    l0 = dict(mix_norm=row(mix_norm[0]), w_in_ab=w_in_ab[0].astype(BF16), pool_w=pool_w[0].astype(BF16),
              pool_scale=row(pool_scale[0]), conv_w=conv_w[0], conv_b=row(conv_b[0]),
              conv_norm_g=row(conv_norm_g[0]), conv_norm_b=row(conv_norm_b[0]),
              w_out_ab=w_out_ab[0].astype(BF16),
              ffn_norm=row(ffn_norm[0]), w_up=w_up[0].astype(BF16), ffn_conv_w=ffn_conv_w[0],
              ffn_conv_b=row(ffn_conv_b[0]), w_down=w_down[0].astype(BF16), final_norm=fin)
    sw_short = jnp.repeat(jnp.transpose(sgu_w[0][:, :dec_seq, :dec_seq], (1, 2, 0)), dh, axis=2)
    l1 = dict(mix_norm=row(mix_norm[1]), w_in_c=w_in_c[0].astype(BF16),
              sgu_norm_g=row(sgu_norm_g[0]), sgu_norm_b=row(sgu_norm_b[0]),
              sgu_w=sgu_w[0], sgu_b_chunk=jnp.repeat(sgu_bias[0].T, dh, axis=1),
              sgu_w_short=sw_short.reshape(dec_seq * dec_seq, d_c),
              sgu_b_short=jnp.repeat(sgu_bias[0][:, :dec_seq].T, dh, axis=1),
              w_out_c=w_out_c[0].astype(BF16),
              ffn_norm=row(ffn_norm[1]), w_up=w_up[1].astype(BF16), ffn_conv_w=ffn_conv_w[1],
              ffn_conv_b=row(ffn_conv_b[1]), w_down=w_down[1].astype(BF16), final_norm=fin)

    xp, pool_p, conv_p = _mixer_ab(x_prompt, None, l0, 64, batch_p, True, 0, "mixer_ab_prompt")
    xp, f0_p = _ffn(xp, None, l0, 512, 1, False, "ffn0_prompt")
    xp, v_p = _mixer_c(xp, l1, 512, 1, "mixer_c_prompt")
    yp, f1_p = _ffn(xp, None, l1, 512, 1, True, "ffn1_prompt")

    bb_s = 64
    xs, pool_s, conv_s = _mixer_ab(_tm(x_sample), (_tm(state_pool[0]), _tm(state_conv[0])), l0,
                                   dec_seq, bb_s, False, PAST_LEN, "mixer_ab_sample")
    xs, f0_s = _ffn(xs, _tm(state_ffn[0]), l0, dec_seq, bb_s, False, "ffn0_sample")
    xs, v_s = _mixer_c(xs, l1, dec_seq, bb_s, "mixer_c_sample")
    ys, f1_s = _ffn(xs, _tm(state_ffn[1]), l1, dec_seq, bb_s, True, "ffn1_sample")

    return (yp, _tm(ys), _tm(pool_p)[None], _tm(pool_s)[None], _tm(conv_p)[None], _tm(conv_s)[None],
            v_p[None], _tm(v_s)[None],
            jnp.stack([f0_p, f1_p]), jnp.stack([_tm(f0_s), _tm(f1_s)]))
```

```python
import functools

import jax
import jax.numpy as jnp
import numpy as np
from jax import lax
from jax.experimental import pallas as pl
from jax.experimental.pallas import tpu as pltpu

F32 = jnp.float32
BF16 = jnp.bfloat16

EPS = 1e-6
POOL_WINDOWS = (2, 4, 8, 16)
POOL_BUF = max(POOL_WINDOWS) - 1
CHUNK = 128
PAST_LEN = 16384
LANES = 128
SUBLANES = 8
FF_CHUNK = 256
DOWN_GROUP = 4
UP_AHEAD = 3
CONV_ROWS = 32
VMEM_LIMIT = 56 * 1024 * 1024


def _rmsnorm(x, g):
    ms = jnp.mean(x * x, axis=-1, keepdims=True)
    return x * lax.rsqrt(ms + EPS) * g


def _layernorm(x, g, b):
    mu = jnp.mean(x, axis=-1, keepdims=True)
    xc = x - mu
    var = jnp.mean(xc * xc, axis=-1, keepdims=True)
    return xc * lax.rsqrt(var + EPS) * g + b


def _silu(x):
    return x * jax.nn.sigmoid(x)


def _bdot(a, b):
    return jnp.dot(a, b, preferred_element_type=F32)


def _ab_front(h, win_ref, pw_ref, ps_ref, xa_full, bb, pos0):
    rows = h.shape[0]
    d_a = xa_full.shape[1]
    hp = POOL_BUF * bb
    proj = _bdot(h, win_ref[...])
    d_b = (proj.shape[1] - d_a) // 2
    xa = proj[:, :d_a]
    glu = proj[:, d_a:d_a + d_b] * jax.nn.sigmoid(proj[:, d_a + d_b:])
    xa_full[hp:hp + rows, :] = xa
    pos = pos0 + lax.broadcasted_iota(jnp.int32, (rows, LANES), 0) // bb
    group = d_a // len(POOL_WINDOWS)
    ya = []
    for gi, w in enumerate(POOL_WINDOWS):
        c0 = gi * group
        xg = xa[:, c0:c0 + group]
        win = xg
        for k in range(1, w):
            win = win + xa_full[hp - k * bb:hp - k * bb + rows, c0:c0 + group]
        cnt = jnp.minimum(pos + 1, w).astype(F32)
        d = (win / cnt - xg).astype(BF16)
        ya.append((_bdot(d, pw_ref[gi]) * ps_ref[:, c0:c0 + group]).astype(BF16))
    return glu, jnp.concatenate(ya, axis=1)


def _conv_rows(glu_full, r0, bb, cw_ref, bias_row):
    acc = jnp.broadcast_to(bias_row, (CONV_ROWS, glu_full.shape[-1]))
    for k in range(cw_ref.shape[0]):
        acc = acc + glu_full[pl.ds(r0 + k * bb, CONV_ROWS), :] * cw_ref[k:k + 1, :]
    return acc


def _mixer_ab_kernel(tt, bb, has_state, start_pos, *refs):
    x_ref, refs = refs[0], refs[1:]
    if has_state:
        sp_ref, sc_ref = refs[:2]
    else:
        p_ref, pt_ref = refs[:2]
    (g_ref, win_ref, pw_ref, ps_ref, cw_ref, cb_ref, lg_ref, lb_ref, wout_ref,
     xo_ref, po_ref, co_ref, xa_full, glu_full, c_ref) = refs[2:]
    rows = tt * bb
    hp = POOL_BUF * bb
    hc = (cw_ref.shape[0] - 1) * bb
    ti = pl.program_id(1)

    if has_state:
        xa_full[0:hp, :] = sp_ref[...].reshape(hp, xa_full.shape[1])
        glu_full[0:hc, :] = sc_ref[...].reshape(hc, glu_full.shape[1])
    else:
        @pl.when(ti == 0)
        def _():
            xa_full[...] = jnp.zeros(xa_full.shape, F32)
            glu_full[...] = jnp.zeros(glu_full.shape, F32)

        xa_full[0:hp, :] = xa_full[rows:rows + hp, :]
        glu_full[0:hc, :] = glu_full[rows:rows + hc, :]

    x = x_ref[...].reshape(rows, x_ref.shape[2])
    h = _rmsnorm(x, g_ref[...]).astype(BF16)
    if not has_state:
        h = _bdot(p_ref[...], h).astype(BF16)
    glu, ya = _ab_front(h, win_ref, pw_ref, ps_ref, xa_full, bb, start_pos + ti * tt)
    glu_full[hc:hc + rows, :] = glu
    po_ref[...] = xa_full[rows:rows + hp, :].reshape(po_ref.shape)
    co_ref[...] = glu_full[rows:rows + hc, :].reshape(co_ref.shape)

    def conv_block(r, carry):
        r0 = pl.multiple_of(r * CONV_ROWS, CONV_ROWS)
        c_ref[pl.ds(r0, CONV_ROWS), :] = _conv_rows(glu_full, r0, bb, cw_ref, cb_ref[...])
        return carry

    lax.fori_loop(0, rows // CONV_ROWS, conv_block, 0)
    yb = _silu(_layernorm(c_ref[...], lg_ref[...], lb_ref[...])).astype(BF16)
    y = jnp.concatenate([ya, yb], axis=1)
    if not has_state:
        y = _bdot(pt_ref[...], y).astype(BF16)
    out = x + _bdot(y, wout_ref[...])
    xo_ref[...] = out.reshape(xo_ref.shape)


def _ffn_conv_slabs(full_ref, up, halo_rows, fw, fb, bb, rows):
    halo = halo_rows.shape[0]
    full_ref[0:halo, :] = halo_rows
    full_ref[halo:halo + rows, :] = up
    tail = full_ref[rows:rows + halo, :]
    cv = (full_ref[0:rows, :] * fw[0:1, :] + full_ref[bb:bb + rows, :] * fw[1:2, :]
          + up * fw[2:3, :] + fb)
    return cv, tail


def _ffn_conv_rolled(up, prev8, fw, fb):
    rows, width = up.shape
    groups = up.reshape(rows // SUBLANES, SUBLANES, width)
    ext = jnp.concatenate([prev8[None], groups], axis=0)
    sub = lax.broadcasted_iota(jnp.int32, (1, SUBLANES, width), 1)
    cv = groups * fw[2:3, :] + fb
    for shift in (1, 2):
        rot = pltpu.roll(ext, shift, 1)
        cv = cv + jnp.where(sub < shift, rot[:-1], rot[1:]) * fw[2 - shift:3 - shift, :]
    return cv.reshape(rows, width)


def _ffn_kernel(tt, bb, has_state, final_norm, *refs):
    x_ref, refs = refs[0], refs[1:]
    if has_state:
        st_ref, refs = refs[0], refs[1:]
    (g_ref, wup_ref, fw_ref, fb_ref, wdown_ref, fin_ref,
     xo_ref, so_ref, h_ref, acc_ref, full_a_ref, full_g_ref, carry_ref) = refs
    rows = tt * bb
    d_ff = wdown_ref.shape[0]
    n_keep = fw_ref.shape[0] - 1
    ti = pl.program_id(1)

    x = x_ref[...].reshape(rows, x_ref.shape[2])
    h_ref[...] = _rmsnorm(x, g_ref[...]).astype(BF16)
    acc_ref[...] = x

    if not has_state:
        @pl.when(ti == 0)
        def _():
            carry_ref[...] = jnp.zeros(carry_ref.shape, F32)

    def up_pair(c):
        return [_bdot(h_ref[...], wup_ref[:, off:off + FF_CHUNK])
                for off in (c * FF_CHUNK, d_ff + c * FF_CHUNK)]

    n_chunks = d_ff // FF_CHUNK
    ups_ahead = [up_pair(c) for c in range(UP_AHEAD)]
    acts = []
    for c in range(n_chunks):
        ups = ups_ahead.pop(0)
        if c + UP_AHEAD < n_chunks:
            ups_ahead.append(up_pair(c + UP_AHEAD))
        col_pair = [slice(off, off + FF_CHUNK) for off in (c * FF_CHUNK, d_ff + c * FF_CHUNK)]
        conv = []
        for full_ref, cols, up in zip((full_a_ref, full_g_ref), col_pair, ups):
            if bb == 1:
                conv.append(_ffn_conv_rolled(up, carry_ref[:, cols], fw_ref[:, cols], fb_ref[:, cols]))
                carry_ref[:, cols] = up[rows - SUBLANES:, :]
                so_ref[0, :, cols] = up[rows - n_keep:, :]
            else:
                if has_state:
                    halo_rows = st_ref[:, :, cols].reshape(n_keep * bb, FF_CHUNK)
                else:
                    halo_rows = carry_ref[:, cols]
                cv, tail = _ffn_conv_slabs(full_ref, up, halo_rows, fw_ref[:, cols], fb_ref[:, cols],
                                           bb, rows)
                so_ref[:, :, cols] = tail.reshape(n_keep, bb, FF_CHUNK)
                if not has_state:
                    carry_ref[:, cols] = tail
                conv.append(cv)
        acts.append((_silu(conv[0]) * conv[1]).astype(BF16))
        if len(acts) == DOWN_GROUP or c + 1 == n_chunks:
            lo = (c + 1 - len(acts)) * FF_CHUNK
            act = acts[0] if len(acts) == 1 else jnp.concatenate(acts, axis=1)
            acc_ref[...] += _bdot(act, wdown_ref[lo:(c + 1) * FF_CHUNK, :])
            acts = []

    out = acc_ref[...]
    if final_norm:
        out = _rmsnorm(out, fin_ref[...])
    xo_ref[...] = out.reshape(xo_ref.shape)


def _mixer_c_kernel(tt, bb, *refs):
    (x_ref, g_ref, win_ref, lg_ref, lb_ref, sw_ref, sb_ref, wout_ref,
     xo_ref, vo_ref, v_ref, vb_ref, z_ref) = refs
    rows = tt * bb
    d_c = wout_ref.shape[0]

    x = x_ref[...].reshape(rows, x_ref.shape[2])
    h = _rmsnorm(x, g_ref[...]).astype(BF16)
    proj = _bdot(h, win_ref[...])
    u = proj[:, :d_c]
    v = _layernorm(proj[:, d_c:], lg_ref[...], lb_ref[...])

    if bb == 1:
        n_heads = sw_ref.shape[0]
        dh = d_c // n_heads
        n_chunks = tt // CHUNK
        vo_ref[0] = v[rows - CHUNK:, :]
        vb_ref[...] = v.astype(BF16)
        row_i = lax.broadcasted_iota(jnp.int32, (CHUNK, CHUNK), 0)
        col_i = lax.broadcasted_iota(jnp.int32, (CHUNK, CHUNK), 1)
        for hd in range(n_heads):
            hcols = slice(hd * dh, (hd + 1) * dh)
            m = jnp.where(col_i <= row_i, sw_ref[hd], 0.0).astype(BF16)
            rhs = jnp.concatenate(
                [vb_ref[c * CHUNK:(c + 1) * CHUNK, hcols] for c in range(n_chunks)], axis=1)
            zc = _bdot(m, rhs)
            bias = sb_ref[:, hcols]
            for c in range(n_chunks):
                z_ref[c * CHUNK:(c + 1) * CHUNK, hcols] = zc[:, c * dh:(c + 1) * dh] + bias
    else:
        vo_ref[...] = v.reshape(vo_ref.shape)
        v_ref[...] = v
        for i in range(tt):
            zi = jnp.broadcast_to(sb_ref[i:i + 1, :], (bb, d_c))
            for j in range(i + 1):
                zi = zi + v_ref[j * bb:(j + 1) * bb, :] * sw_ref[i * tt + j:i * tt + j + 1, :]
            z_ref[i * bb:(i + 1) * bb, :] = zi

    gated = (u * z_ref[...]).astype(BF16)
    out = x + _bdot(gated, wout_ref[...])
    xo_ref[...] = out.reshape(xo_ref.shape)


def _const_spec(shape):
    zeros = (0,) * len(shape)
    return pl.BlockSpec(shape, lambda bi, ti: zeros, pipeline_mode=pl.Buffered(1))


def _tm_spec(tt, bb, width):
    return pl.BlockSpec((tt, bb, width), lambda bi, ti: (ti, bi, 0))


def _bm_spec(bb, tt, width):
    return pl.BlockSpec((bb, tt, width), lambda bi, ti: (bi, ti, 0))


def _tm_state_spec(n, bb, width):
    return pl.BlockSpec((n, bb, width), lambda bi, ti: (0, bi, 0))


_PARAMS = pltpu.CompilerParams(
    dimension_semantics=("arbitrary", "arbitrary"), vmem_limit_bytes=VMEM_LIMIT)


def _row_permutation(tt, bb):
    m = np.arange(tt * bb)
    src = (m % bb) * tt + m // bb
    return jnp.asarray(src[:, None] == m[None, :], BF16)


def _ab_weights(w):
    return (w["mix_norm"], w["w_in_ab"], w["pool_w"], w["pool_scale"], w["conv_w"], w["conv_b"],
            w["conv_norm_g"], w["conv_norm_b"], w["w_out_ab"])


def _mixer_ab(x, states, w, tt, bb, start_pos, name):
    has_state = states is not None
    if has_state:
        t_len, batch, d_model = x.shape
        x_spec = _tm_spec(tt, bb, d_model)
    else:
        batch, t_len, d_model = x.shape
        x_spec = _bm_spec(bb, tt, d_model)
        assert bb == batch and bb % SUBLANES == 0
    d_a = w["pool_scale"].shape[1]
    d_b = w["conv_b"].shape[1]
    conv_k = w["conv_w"].shape[0]
    assert t_len == tt or tt >= conv_k - 1
    rows = tt * bb
    if has_state:
        extra = tuple(states)
        extra_specs = [_tm_state_spec(POOL_BUF, bb, d_a), _tm_state_spec(conv_k - 1, bb, d_b)]
    else:
        perm = _row_permutation(tt, bb)
        extra = (perm, perm.T)
        extra_specs = [_const_spec(perm.shape)] * 2
    weights = _ab_weights(w)
    return pl.pallas_call(
        functools.partial(_mixer_ab_kernel, tt, bb, has_state, start_pos),
        grid=(batch // bb, t_len // tt),
        in_specs=[x_spec] + extra_specs + [_const_spec(a.shape) for a in weights],
        out_specs=[x_spec, _tm_state_spec(POOL_BUF, bb, d_a), _tm_state_spec(conv_k - 1, bb, d_b)],
        out_shape=[jax.ShapeDtypeStruct(x.shape, F32),
                   jax.ShapeDtypeStruct((POOL_BUF, batch, d_a), F32),
                   jax.ShapeDtypeStruct((conv_k - 1, batch, d_b), F32)],
        scratch_shapes=[pltpu.VMEM(((tt + POOL_BUF) * bb, d_a), F32),
                        pltpu.VMEM(((tt + conv_k - 1) * bb, d_b), F32),
                        pltpu.VMEM((rows, d_b), F32)],
        compiler_params=_PARAMS,
        name=name,
    )(x, *extra, *weights)


def _ffn(x, state, w, tt, bb, final_norm, name):
    d_ff = w["w_down"].shape[0]
    n_keep = w["ffn_conv_w"].shape[0] - 1
    has_state = state is not None
    if bb == 1:
        assert not has_state and tt % SUBLANES == 0
        batch, t_len, d_model = x.shape
        x_spec = _bm_spec(1, tt, d_model)
        st_spec = pl.BlockSpec((1, n_keep, 2 * d_ff), lambda bi, ti: (bi, 0, 0))
        st_shape = (batch, n_keep, 2 * d_ff)
        carry_rows = SUBLANES
    else:
        t_len, batch, d_model = x.shape
        x_spec = _tm_spec(tt, bb, d_model)
        st_spec = _tm_state_spec(n_keep, bb, 2 * d_ff)
        st_shape = (n_keep, batch, 2 * d_ff)
        carry_rows = n_keep * bb
    rows = tt * bb
    weights = (w["ffn_norm"], w["w_up"], w["ffn_conv_w"], w["ffn_conv_b"], w["w_down"], w["final_norm"])
    in_specs = [x_spec] + ([st_spec] if has_state else []) + [_const_spec(a.shape) for a in weights]
    return pl.pallas_call(
        functools.partial(_ffn_kernel, tt, bb, has_state, final_norm),
        grid=(batch // bb, t_len // tt),
        in_specs=in_specs,
        out_specs=[x_spec, st_spec],
        out_shape=[jax.ShapeDtypeStruct(x.shape, F32), jax.ShapeDtypeStruct(st_shape, F32)],
        scratch_shapes=[pltpu.VMEM((rows, d_model), BF16),
                        pltpu.VMEM((rows, d_model), F32),
                        pltpu.VMEM((rows + carry_rows, FF_CHUNK), F32),
                        pltpu.VMEM((rows + carry_rows, FF_CHUNK), F32),
                        pltpu.VMEM((carry_rows, 2 * d_ff), F32)],
        compiler_params=_PARAMS,
        name=name,
    )(x, *((state,) if has_state else ()), *weights)


def _mixer_c(x, w, tt, bb, name):
    d_c = w["w_out_c"].shape[0]
    if bb == 1:
        assert tt % CHUNK == 0
        batch, t_len, d_model = x.shape
        x_spec = _bm_spec(1, tt, d_model)
        v_spec = pl.BlockSpec((1, CHUNK, d_c), lambda bi, ti: (bi, 0, 0))
        v_shape = (batch, CHUNK, d_c)
        sgu = (w["sgu_w"], w["sgu_b_chunk"])
    else:
        t_len, batch, d_model = x.shape
        assert t_len == tt
        x_spec = _tm_spec(tt, bb, d_model)
        v_spec = _tm_spec(tt, bb, d_c)
        v_shape = (t_len, batch, d_c)
        sgu = (w["sgu_w_short"], w["sgu_b_short"])
    rows = tt * bb
    weights = (w["mix_norm"], w["w_in_c"], w["sgu_norm_g"], w["sgu_norm_b"]) + sgu + (w["w_out_c"],)
    return pl.pallas_call(
        functools.partial(_mixer_c_kernel, tt, bb),
        grid=(batch // bb, t_len // tt),
        in_specs=[x_spec] + [_const_spec(a.shape) for a in weights],
        out_specs=[x_spec, v_spec],
        out_shape=[jax.ShapeDtypeStruct(x.shape, F32), jax.ShapeDtypeStruct(v_shape, F32)],
        scratch_shapes=[pltpu.VMEM((rows, d_c), F32),
                        pltpu.VMEM((rows, d_c), BF16),
                        pltpu.VMEM((rows, d_c), F32)],
        compiler_params=_PARAMS,
        name=name,
    )(x, *weights)


def _tm(a):
    return jnp.swapaxes(a, 0, 1)


def kernel(x_prompt, x_sample, state_pool, state_conv, state_ffn, mix_norm, w_in_ab, pool_w, pool_scale, conv_w, conv_b, conv_norm_g, conv_norm_b, w_out_ab, w_in_c, sgu_norm_g, sgu_norm_b, sgu_w, sgu_bias, w_out_c, ffn_norm, w_up, ffn_conv_w, ffn_conv_b, w_down, final_norm):
    batch_p = x_prompt.shape[0]
    dec_seq = x_sample.shape[1]
    n_heads = sgu_w.shape[1]
    d_c = w_out_c.shape[1]
    dh = d_c // n_heads
    row = lambda a: a.reshape(1, -1)
    fin = row(final_norm)
    l0 = dict(mix_norm=row(mix_norm[0]), w_in_ab=w_in_ab[0].astype(BF16), pool_w=pool_w[0].astype(BF16),
              pool_scale=row(pool_scale[0]), conv_w=conv_w[0], conv_b=row(conv_b[0]),
              conv_norm_g=row(conv_norm_g[0]), conv_norm_b=row(conv_norm_b[0]),
              w_out_ab=w_out_ab[0].astype(BF16),
              ffn_norm=row(ffn_norm[0]), w_up=w_up[0].astype(BF16), ffn_conv_w=ffn_conv_w[0],
              ffn_conv_b=row(ffn_conv_b[0]), w_down=w_down[0].astype(BF16), final_norm=fin)
    sw_short = jnp.repeat(jnp.transpose(sgu_w[0][:, :dec_seq, :dec_seq], (1, 2, 0)), dh, axis=2)
    l1 = dict(mix_norm=row(mix_norm[1]), w_in_c=w_in_c[0].astype(BF16),
              sgu_norm_g=row(sgu_norm_g[0]), sgu_norm_b=row(sgu_norm_b[0]),
              sgu_w=sgu_w[0], sgu_b_chunk=jnp.repeat(sgu_bias[0].T, dh, axis=1),
              sgu_w_short=sw_short.reshape(dec_seq * dec_seq, d_c),
              sgu_b_short=jnp.repeat(sgu_bias[0][:, :dec_seq].T, dh, axis=1),
              w_out_c=w_out_c[0].astype(BF16),
              ffn_norm=row(ffn_norm[1]), w_up=w_up[1].astype(BF16), ffn_conv_w=ffn_conv_w[1],
              ffn_conv_b=row(ffn_conv_b[1]), w_down=w_down[1].astype(BF16), final_norm=fin)

    xp, pool_p, conv_p = _mixer_ab(x_prompt, None, l0, 64, batch_p, 0, "mixer_ab_prompt")
    xp, f0_p = _ffn(xp, None, l0, 512, 1, False, "ffn0_prompt")
    xp, v_p = _mixer_c(xp, l1, 512, 1, "mixer_c_prompt")
    yp, f1_p = _ffn(xp, None, l1, 512, 1, True, "ffn1_prompt")

    bb_s = 64
    xs, pool_s, conv_s = _mixer_ab(_tm(x_sample), (_tm(state_pool[0]), _tm(state_conv[0])), l0,
                                   dec_seq, bb_s, PAST_LEN, "mixer_ab_sample")
    xs, f0_s = _ffn(xs, _tm(state_ffn[0]), l0, dec_seq, bb_s, False, "ffn0_sample")
    xs, v_s = _mixer_c(xs, l1, dec_seq, bb_s, "mixer_c_sample")
    ys, f1_s = _ffn(xs, _tm(state_ffn[1]), l1, dec_seq, bb_s, True, "ffn1_sample")

    return (yp, _tm(ys), _tm(pool_p)[None], _tm(pool_s)[None], _tm(conv_p)[None], _tm(conv_s)[None],
            v_p[None], _tm(v_s)[None],
            jnp.stack([f0_p, f1_p]), jnp.stack([_tm(f0_s), _tm(f1_s)]))
```

```python
import functools

import jax
import jax.numpy as jnp
import numpy as np
from jax import lax
from jax.experimental import pallas as pl
from jax.experimental.pallas import tpu as pltpu

F32 = jnp.float32
BF16 = jnp.bfloat16

EPS = 1e-6
POOL_WINDOWS = (2, 4, 8, 16)
POOL_BUF = max(POOL_WINDOWS) - 1
CHUNK = 128
PAST_LEN = 16384
LANES = 128
SUBLANES = 8
FF_CHUNK = 256
DOWN_GROUP = 4
UP_AHEAD = 3
CONV_ROWS = 64
VMEM_LIMIT = 56 * 1024 * 1024


def _rmsnorm(x, g):
    ms = jnp.mean(x * x, axis=-1, keepdims=True)
    return x * lax.rsqrt(ms + EPS) * g


def _layernorm(x, g, b):
    mu = jnp.mean(x, axis=-1, keepdims=True)
    xc = x - mu
    var = jnp.mean(xc * xc, axis=-1, keepdims=True)
    return xc * lax.rsqrt(var + EPS) * g + b


def _silu(x):
    return x * jax.nn.sigmoid(x)


def _bdot(a, b):
    return jnp.dot(a, b, preferred_element_type=F32)


def _ab_front(h, win_ref, pw_ref, ps_ref, xa_full, bb, pos0):
    rows = h.shape[0]
    d_a = xa_full.shape[1]
    hp = POOL_BUF * bb
    proj = _bdot(h, win_ref[...])
    d_b = (proj.shape[1] - d_a) // 2
    xa = proj[:, :d_a]
    glu = proj[:, d_a:d_a + d_b] * jax.nn.sigmoid(proj[:, d_a + d_b:])
    xa_full[hp:hp + rows, :] = xa
    pos = pos0 + lax.broadcasted_iota(jnp.int32, (rows, LANES), 0) // bb
    group = d_a // len(POOL_WINDOWS)
    ya = []
    for gi, w in enumerate(POOL_WINDOWS):
        c0 = gi * group
        xg = xa[:, c0:c0 + group]
        win = xg
        for k in range(1, w):
            win = win + xa_full[hp - k * bb:hp - k * bb + rows, c0:c0 + group]
        cnt = jnp.minimum(pos + 1, w).astype(F32)
        d = (win / cnt - xg).astype(BF16)
        ya.append((_bdot(d, pw_ref[gi]) * ps_ref[:, c0:c0 + group]).astype(BF16))
    return glu, jnp.concatenate(ya, axis=1)


def _conv_rows(glu_full, r0, bb, cw_ref, bias_row):
    acc = jnp.broadcast_to(bias_row, (CONV_ROWS, glu_full.shape[-1]))
    for k in range(cw_ref.shape[0]):
        acc = acc + glu_full[pl.ds(r0 + k * bb, CONV_ROWS), :] * cw_ref[k:k + 1, :]
    return acc


def _mixer_ab_kernel(tt, bb, has_state, start_pos, *refs):
    x_ref, refs = refs[0], refs[1:]
    if has_state:
        sp_ref, sc_ref = refs[:2]
    else:
        p_ref, pt_ref = refs[:2]
    (g_ref, win_ref, pw_ref, ps_ref, cw_ref, cb_ref, lg_ref, lb_ref, wout_ref,
     xo_ref, po_ref, co_ref, xa_full, glu_full, c_ref) = refs[2:]
    rows = tt * bb
    hp = POOL_BUF * bb
    hc = (cw_ref.shape[0] - 1) * bb
    ti = pl.program_id(1)

    if has_state:
        xa_full[0:hp, :] = sp_ref[...].reshape(hp, xa_full.shape[1])
        glu_full[0:hc, :] = sc_ref[...].reshape(hc, glu_full.shape[1])
    else:
        @pl.when(ti == 0)
        def _():
            xa_full[...] = jnp.zeros(xa_full.shape, F32)
            glu_full[...] = jnp.zeros(glu_full.shape, F32)

        xa_full[0:hp, :] = xa_full[rows:rows + hp, :]
        glu_full[0:hc, :] = glu_full[rows:rows + hc, :]

    x = x_ref[...].reshape(rows, x_ref.shape[2])
    h = _rmsnorm(x, g_ref[...]).astype(BF16)
    if not has_state:
        h = _bdot(p_ref[...], h).astype(BF16)
    glu, ya = _ab_front(h, win_ref, pw_ref, ps_ref, xa_full, bb, start_pos + ti * tt)
    glu_full[hc:hc + rows, :] = glu
    po_ref[...] = xa_full[rows:rows + hp, :].reshape(po_ref.shape)
    co_ref[...] = glu_full[rows:rows + hc, :].reshape(co_ref.shape)

    def conv_block(r, carry):
        r0 = pl.multiple_of(r * CONV_ROWS, CONV_ROWS)
        c_ref[pl.ds(r0, CONV_ROWS), :] = _conv_rows(glu_full, r0, bb, cw_ref, cb_ref[...])
        return carry

    lax.fori_loop(0, rows // CONV_ROWS, conv_block, 0)
    yb = _silu(_layernorm(c_ref[...], lg_ref[...], lb_ref[...])).astype(BF16)
    y = jnp.concatenate([ya, yb], axis=1)
    if not has_state:
        y = _bdot(pt_ref[...], y).astype(BF16)
    out = x + _bdot(y, wout_ref[...])
    xo_ref[...] = out.reshape(xo_ref.shape)


def _ffn_conv_slabs(full_ref, up, halo_rows, fw, fb, bb, rows):
    halo = halo_rows.shape[0]
    full_ref[0:halo, :] = halo_rows
    full_ref[halo:halo + rows, :] = up
    tail = full_ref[rows:rows + halo, :]
    cv = (full_ref[0:rows, :] * fw[0:1, :] + full_ref[bb:bb + rows, :] * fw[1:2, :]
          + up * fw[2:3, :] + fb)
    return cv, tail


def _ffn_conv_rolled(up, prev8, fw, fb):
    rows, width = up.shape
    groups = up.reshape(rows // SUBLANES, SUBLANES, width)
    ext = jnp.concatenate([prev8[None], groups], axis=0)
    sub = lax.broadcasted_iota(jnp.int32, (1, SUBLANES, width), 1)
    cv = groups * fw[2:3, :] + fb
    for shift in (1, 2):
        rot = pltpu.roll(ext, shift, 1)
        cv = cv + jnp.where(sub < shift, rot[:-1], rot[1:]) * fw[2 - shift:3 - shift, :]
    return cv.reshape(rows, width)


def _ffn_kernel(tt, bb, has_state, final_norm, *refs):
    x_ref, refs = refs[0], refs[1:]
    if has_state:
        st_ref, refs = refs[0], refs[1:]
    (g_ref, wup_ref, fw_ref, fb_ref, wdown_ref, fin_ref,
     xo_ref, so_ref, h_ref, acc_ref, full_a_ref, full_g_ref, carry_ref) = refs
    rows = tt * bb
    d_ff = wdown_ref.shape[0]
    n_keep = fw_ref.shape[0] - 1
    ti = pl.program_id(1)

    d_model = x_ref.shape[2]
    h_ref[...] = _rmsnorm(x_ref[...].reshape(rows, d_model), g_ref[...]).astype(BF16)

    if not has_state:
        @pl.when(ti == 0)
        def _():
            carry_ref[...] = jnp.zeros(carry_ref.shape, F32)

    def up_pair(c):
        return [_bdot(h_ref[...], wup_ref[:, off:off + FF_CHUNK])
                for off in (c * FF_CHUNK, d_ff + c * FF_CHUNK)]

    n_chunks = d_ff // FF_CHUNK
    ups_ahead = [up_pair(c) for c in range(UP_AHEAD)]
    acts = []
    for c in range(n_chunks):
        ups = ups_ahead.pop(0)
        if c + UP_AHEAD < n_chunks:
            ups_ahead.append(up_pair(c + UP_AHEAD))
        col_pair = [slice(off, off + FF_CHUNK) for off in (c * FF_CHUNK, d_ff + c * FF_CHUNK)]
        conv = []
        for full_ref, cols, up in zip((full_a_ref, full_g_ref), col_pair, ups):
            if bb == 1:
                conv.append(_ffn_conv_rolled(up, carry_ref[:, cols], fw_ref[:, cols], fb_ref[:, cols]))
                carry_ref[:, cols] = up[rows - SUBLANES:, :]
                so_ref[0, :, cols] = up[rows - n_keep:, :]
            else:
                if has_state:
                    halo_rows = st_ref[:, :, cols].reshape(n_keep * bb, FF_CHUNK)
                else:
                    halo_rows = carry_ref[:, cols]
                cv, tail = _ffn_conv_slabs(full_ref, up, halo_rows, fw_ref[:, cols], fb_ref[:, cols],
                                           bb, rows)
                so_ref[:, :, cols] = tail.reshape(n_keep, bb, FF_CHUNK)
                if not has_state:
                    carry_ref[:, cols] = tail
                conv.append(cv)
        acts.append((_silu(conv[0]) * conv[1]).astype(BF16))
        if len(acts) == DOWN_GROUP or c + 1 == n_chunks:
            lo = (c + 1 - len(acts)) * FF_CHUNK
            act = acts[0] if len(acts) == 1 else jnp.concatenate(acts, axis=1)
            base = x_ref[...].reshape(rows, d_model) if lo == 0 else acc_ref[...]
            total = base + _bdot(act, wdown_ref[lo:(c + 1) * FF_CHUNK, :])
            if c + 1 < n_chunks:
                acc_ref[...] = total
            else:
                if final_norm:
                    total = _rmsnorm(total, fin_ref[...])
                xo_ref[...] = total.reshape(xo_ref.shape)
            acts = []


def _mixer_c_kernel(tt, bb, *refs):
    (x_ref, g_ref, win_ref, lg_ref, lb_ref, sw_ref, sb_ref, wout_ref,
     xo_ref, vo_ref, v_ref, vb_ref, z_ref) = refs
    rows = tt * bb
    d_c = wout_ref.shape[0]

    x = x_ref[...].reshape(rows, x_ref.shape[2])
    h = _rmsnorm(x, g_ref[...]).astype(BF16)
    v = _layernorm(_bdot(h, win_ref[:, d_c:]), lg_ref[...], lb_ref[...])
    u = _bdot(h, win_ref[:, :d_c])

    if bb == 1:
        n_heads = sw_ref.shape[0]
        dh = d_c // n_heads
        n_chunks = tt // CHUNK
        vo_ref[0] = v[rows - CHUNK:, :]
        vb_ref[...] = v.astype(BF16)
        row_i = lax.broadcasted_iota(jnp.int32, (CHUNK, CHUNK), 0)
        col_i = lax.broadcasted_iota(jnp.int32, (CHUNK, CHUNK), 1)
        for hd in range(n_heads):
            hcols = slice(hd * dh, (hd + 1) * dh)
            m = jnp.where(col_i <= row_i, sw_ref[hd], 0.0).astype(BF16)
            rhs = jnp.concatenate(
                [vb_ref[c * CHUNK:(c + 1) * CHUNK, hcols] for c in range(n_chunks)], axis=1)
            zc = _bdot(m, rhs)
            bias = sb_ref[:, hcols]
            for c in range(n_chunks):
                z_ref[c * CHUNK:(c + 1) * CHUNK, hcols] = zc[:, c * dh:(c + 1) * dh] + bias
    else:
        vo_ref[...] = v.reshape(vo_ref.shape)
        v_ref[...] = v
        for i in range(tt):
            zi = jnp.broadcast_to(sb_ref[i:i + 1, :], (bb, d_c))
            for j in range(i + 1):
                zi = zi + v_ref[j * bb:(j + 1) * bb, :] * sw_ref[i * tt + j:i * tt + j + 1, :]
            z_ref[i * bb:(i + 1) * bb, :] = zi

    gated = (u * z_ref[...]).astype(BF16)
    out = x + _bdot(gated, wout_ref[...])
    xo_ref[...] = out.reshape(xo_ref.shape)


def _layer_shape(entry):
    return entry[0].shape[1:] if isinstance(entry, tuple) else entry.shape


def _resident(entries):
    arrays, specs = [], []
    for entry in entries:
        arr, layer = entry if isinstance(entry, tuple) else (entry, None)
        if layer is None:
            block, index = arr.shape, (0,) * arr.ndim
        else:
            block, index = (None,) + arr.shape[1:], (layer,) + (0,) * (arr.ndim - 1)
        arrays.append(arr)
        specs.append(pl.BlockSpec(block, lambda bi, ti, index=index: index,
                                  pipeline_mode=pl.Buffered(1)))
    return arrays, specs


def _tm_spec(tt, bb, width):
    return pl.BlockSpec((tt, bb, width), lambda bi, ti: (ti, bi, 0))


def _bm_spec(bb, tt, width):
    return pl.BlockSpec((bb, tt, width), lambda bi, ti: (bi, ti, 0))


def _tm_state_spec(n, bb, width):
    return pl.BlockSpec((n, bb, width), lambda bi, ti: (0, bi, 0))


_PARAMS = pltpu.CompilerParams(
    dimension_semantics=("arbitrary", "arbitrary"), vmem_limit_bytes=VMEM_LIMIT)


def _row_permutation(tt, bb):
    m = np.arange(tt * bb)
    src = (m % bb) * tt + m // bb
    return jnp.asarray(src[:, None] == m[None, :], BF16)


def _ab_weights(w):
    return (w["mix_norm"], w["w_in_ab"], w["pool_w"], w["pool_scale"], w["conv_w"], w["conv_b"],
            w["conv_norm_g"], w["conv_norm_b"], w["w_out_ab"])


def _mixer_ab(x, states, w, tt, bb, start_pos, name):
    has_state = states is not None
    if has_state:
        t_len, batch, d_model = x.shape
        x_spec = _tm_spec(tt, bb, d_model)
    else:
        batch, t_len, d_model = x.shape
        x_spec = _bm_spec(bb, tt, d_model)
        assert bb == batch and bb % SUBLANES == 0
    d_a = _layer_shape(w["pool_scale"])[1]
    d_b = _layer_shape(w["conv_b"])[1]
    conv_k = _layer_shape(w["conv_w"])[0]
    assert t_len == tt or tt >= conv_k - 1
    rows = tt * bb
    if has_state:
        extra = list(states)
        extra_specs = [_tm_state_spec(POOL_BUF, bb, d_a), _tm_state_spec(conv_k - 1, bb, d_b)]
    else:
        perm = _row_permutation(tt, bb)
        extra, extra_specs = _resident([perm, perm.T])
    weights, weight_specs = _resident(_ab_weights(w))
    return pl.pallas_call(
        functools.partial(_mixer_ab_kernel, tt, bb, has_state, start_pos),
        grid=(batch // bb, t_len // tt),
        in_specs=[x_spec] + extra_specs + weight_specs,
        out_specs=[x_spec, _tm_state_spec(POOL_BUF, bb, d_a), _tm_state_spec(conv_k - 1, bb, d_b)],
        out_shape=[jax.ShapeDtypeStruct(x.shape, F32),
                   jax.ShapeDtypeStruct((POOL_BUF, batch, d_a), F32),
                   jax.ShapeDtypeStruct((conv_k - 1, batch, d_b), F32)],
        scratch_shapes=[pltpu.VMEM(((tt + POOL_BUF) * bb, d_a), F32),
                        pltpu.VMEM(((tt + conv_k - 1) * bb, d_b), F32),
                        pltpu.VMEM((rows, d_b), F32)],
        compiler_params=_PARAMS,
        name=name,
    )(x, *extra, *weights)


def _ffn(x, state, w, tt, bb, final_norm, name):
    d_ff = _layer_shape(w["w_down"])[0]
    n_keep = _layer_shape(w["ffn_conv_w"])[0] - 1
    has_state = state is not None
    if bb == 1:
        assert not has_state and tt % SUBLANES == 0
        batch, t_len, d_model = x.shape
        x_spec = _bm_spec(1, tt, d_model)
        st_spec = pl.BlockSpec((1, n_keep, 2 * d_ff), lambda bi, ti: (bi, 0, 0))
        st_shape = (batch, n_keep, 2 * d_ff)
        carry_rows = SUBLANES
    else:
        t_len, batch, d_model = x.shape
        x_spec = _tm_spec(tt, bb, d_model)
        st_spec = _tm_state_spec(n_keep, bb, 2 * d_ff)
        st_shape = (n_keep, batch, 2 * d_ff)
        carry_rows = n_keep * bb
    rows = tt * bb
    weights, weight_specs = _resident(
        (w["ffn_norm"], w["w_up"], w["ffn_conv_w"], w["ffn_conv_b"], w["w_down"], w["final_norm"]))
    in_specs = [x_spec] + ([st_spec] if has_state else []) + weight_specs
    return pl.pallas_call(
        functools.partial(_ffn_kernel, tt, bb, has_state, final_norm),
        grid=(batch // bb, t_len // tt),
        in_specs=in_specs,
        out_specs=[x_spec, st_spec],
        out_shape=[jax.ShapeDtypeStruct(x.shape, F32), jax.ShapeDtypeStruct(st_shape, F32)],
        scratch_shapes=[pltpu.VMEM((rows, d_model), BF16),
                        pltpu.VMEM((rows, d_model), F32),
                        pltpu.VMEM((rows + carry_rows, FF_CHUNK), F32),
                        pltpu.VMEM((rows + carry_rows, FF_CHUNK), F32),
                        pltpu.VMEM((carry_rows, 2 * d_ff), F32)],
        compiler_params=_PARAMS,
        name=name,
    )(x, *((state,) if has_state else ()), *weights)


def _mixer_c(x, w, tt, bb, name):
    d_c = _layer_shape(w["w_out_c"])[0]
    if bb == 1:
        assert tt % CHUNK == 0
        batch, t_len, d_model = x.shape
        x_spec = _bm_spec(1, tt, d_model)
        v_spec = pl.BlockSpec((1, CHUNK, d_c), lambda bi, ti: (bi, 0, 0))
        v_shape = (batch, CHUNK, d_c)
        sgu = (w["sgu_w"], w["sgu_b_chunk"])
    else:
        t_len, batch, d_model = x.shape
        assert t_len == tt
        x_spec = _tm_spec(tt, bb, d_model)
        v_spec = _tm_spec(tt, bb, d_c)
        v_shape = (t_len, batch, d_c)
        sgu = (w["sgu_w_short"], w["sgu_b_short"])
    rows = tt * bb
    weights, weight_specs = _resident(
        (w["mix_norm"], w["w_in_c"], w["sgu_norm_g"], w["sgu_norm_b"]) + sgu + (w["w_out_c"],))
    return pl.pallas_call(
        functools.partial(_mixer_c_kernel, tt, bb),
        grid=(batch // bb, t_len // tt),
        in_specs=[x_spec] + weight_specs,
        out_specs=[x_spec, v_spec],
        out_shape=[jax.ShapeDtypeStruct(x.shape, F32), jax.ShapeDtypeStruct(v_shape, F32)],
        scratch_shapes=[pltpu.VMEM((rows, d_c), F32),
                        pltpu.VMEM((rows, d_c), BF16),
                        pltpu.VMEM((rows, d_c), F32)],
        compiler_params=_PARAMS,
        name=name,
    )(x, *weights)


def _tm(a):
    return jnp.swapaxes(a, 0, 1)


def kernel(x_prompt, x_sample, state_pool, state_conv, state_ffn, mix_norm, w_in_ab, pool_w, pool_scale, conv_w, conv_b, conv_norm_g, conv_norm_b, w_out_ab, w_in_c, sgu_norm_g, sgu_norm_b, sgu_w, sgu_bias, w_out_c, ffn_norm, w_up, ffn_conv_w, ffn_conv_b, w_down, final_norm):
    batch_p = x_prompt.shape[0]
    dec_seq = x_sample.shape[1]
    n_heads = sgu_w.shape[1]
    d_c = w_out_c.shape[1]
    dh = d_c // n_heads
    row = lambda a: a.reshape(1, -1)
    rows_of = lambda a: a.reshape(a.shape[0], 1, -1)
    fin = row(final_norm)
    mix_n, ffn_n, ffn_cb = rows_of(mix_norm), rows_of(ffn_norm), rows_of(ffn_conv_b)
    w_up_b, w_down_b = w_up.astype(BF16), w_down.astype(BF16)
    ffn = lambda layer: dict(ffn_norm=(ffn_n, layer), w_up=(w_up_b, layer), ffn_conv_w=(ffn_conv_w, layer),
                             ffn_conv_b=(ffn_cb, layer), w_down=(w_down_b, layer), final_norm=fin)
    l0 = dict(mix_norm=(mix_n, 0), w_in_ab=w_in_ab[0].astype(BF16), pool_w=pool_w[0].astype(BF16),
              pool_scale=row(pool_scale[0]), conv_w=conv_w[0], conv_b=row(conv_b[0]),
              conv_norm_g=row(conv_norm_g[0]), conv_norm_b=row(conv_norm_b[0]),
              w_out_ab=w_out_ab[0].astype(BF16), **ffn(0))
    sw_short = jnp.repeat(jnp.transpose(sgu_w[0][:, :dec_seq, :dec_seq], (1, 2, 0)), dh, axis=2)
    l1 = dict(mix_norm=(mix_n, 1), w_in_c=w_in_c[0].astype(BF16),
              sgu_norm_g=row(sgu_norm_g[0]), sgu_norm_b=row(sgu_norm_b[0]),
              sgu_w=sgu_w[0], sgu_b_chunk=jnp.repeat(sgu_bias[0].T, dh, axis=1),
              sgu_w_short=sw_short.reshape(dec_seq * dec_seq, d_c),
              sgu_b_short=jnp.repeat(sgu_bias[0][:, :dec_seq].T, dh, axis=1),
              w_out_c=w_out_c[0].astype(BF16), **ffn(1))

    xp, pool_p, conv_p = _mixer_ab(x_prompt, None, l0, 64, batch_p, 0, "mixer_ab_prompt")
    xp, f0_p = _ffn(xp, None, l0, 512, 1, False, "ffn0_prompt")
    xp, v_p = _mixer_c(xp, l1, 512, 1, "mixer_c_prompt")
    yp, f1_p = _ffn(xp, None, l1, 512, 1, True, "ffn1_prompt")

    bb_s = 64
    xs, pool_s, conv_s = _mixer_ab(_tm(x_sample), (_tm(state_pool[0]), _tm(state_conv[0])), l0,
                                   dec_seq, bb_s, PAST_LEN, "mixer_ab_sample")
    xs, f0_s = _ffn(xs, _tm(state_ffn[0]), l0, dec_seq, bb_s, False, "ffn0_sample")
    xs, v_s = _mixer_c(xs, l1, dec_seq, bb_s, "mixer_c_sample")
    ys, f1_s = _ffn(xs, _tm(state_ffn[1]), l1, dec_seq, bb_s, True, "ffn1_sample")

    return (yp, _tm(ys), _tm(pool_p)[None], _tm(pool_s)[None], _tm(conv_p)[None], _tm(conv_s)[None],
            v_p[None], _tm(v_s)[None],
            jnp.stack([f0_p, f1_p]), jnp.stack([_tm(f0_s), _tm(f1_s)]))
```

```python
import functools

import jax
import jax.numpy as jnp
import numpy as np
from jax import lax
from jax.experimental import pallas as pl
from jax.experimental.pallas import tpu as pltpu

F32 = jnp.float32
BF16 = jnp.bfloat16

EPS = 1e-6
POOL_WINDOWS = (2, 4, 8, 16)
POOL_BUF = max(POOL_WINDOWS) - 1
CHUNK = 128
PAST_LEN = 16384
LANES = 128
SUBLANES = 8
FF_CHUNK = 256
DOWN_GROUP = 4
UP_AHEAD = 3
CONV_ROWS = 64
VMEM_LIMIT = 56 * 1024 * 1024


def _rmsnorm(x, g):
    ms = jnp.mean(x * x, axis=-1, keepdims=True)
    return x * lax.rsqrt(ms + EPS) * g


def _layernorm(x, g, b):
    mu = jnp.mean(x, axis=-1, keepdims=True)
    xc = x - mu
    var = jnp.mean(xc * xc, axis=-1, keepdims=True)
    return xc * lax.rsqrt(var + EPS) * g + b


def _silu(x):
    return x * jax.nn.sigmoid(x)


def _bdot(a, b):
    return jnp.dot(a, b, preferred_element_type=F32)


def _ab_front(h, win_ref, pw_ref, ps_ref, xa_full, bb, pos0):
    rows = h.shape[0]
    d_a = xa_full.shape[1]
    hp = POOL_BUF * bb
    proj = _bdot(h, win_ref[...])
    d_b = (proj.shape[1] - d_a) // 2
    xa = proj[:, :d_a]
    glu = proj[:, d_a:d_a + d_b] * jax.nn.sigmoid(proj[:, d_a + d_b:])
    xa_full[hp:hp + rows, :] = xa
    pos = pos0 + lax.broadcasted_iota(jnp.int32, (rows, LANES), 0) // bb
    group = d_a // len(POOL_WINDOWS)
    ya = []
    for gi, w in enumerate(POOL_WINDOWS):
        c0 = gi * group
        xg = xa[:, c0:c0 + group]
        win = xg
        for k in range(1, w):
            win = win + xa_full[hp - k * bb:hp - k * bb + rows, c0:c0 + group]
        cnt = jnp.minimum(pos + 1, w).astype(F32)
        d = (win / cnt - xg).astype(BF16)
        ya.append((_bdot(d, pw_ref[gi]) * ps_ref[:, c0:c0 + group]).astype(BF16))
    return glu, jnp.concatenate(ya, axis=1)


def _conv_tile_dense(glu_full, c_ref, r0, lanes, taps, bias):
    n_out = CONV_ROWS // SUBLANES
    acc = [bias] * n_out
    for m in range(n_out + len(taps) - 1):
        grp = glu_full[pl.ds(r0 + m * SUBLANES, SUBLANES), lanes]
        for k in range(len(taps)):
            if 0 <= m - k < n_out:
                acc[m - k] = acc[m - k] + grp * taps[k]
    for i in range(n_out):
        c_ref[pl.ds(r0 + i * SUBLANES, SUBLANES), lanes] = acc[i]


def _depthwise_conv(glu_full, c_ref, rows, bb, cw_ref, cb_ref):
    n_taps = cw_ref.shape[0]
    n_blocks = rows // CONV_ROWS
    if bb != SUBLANES:
        def block(r, carry):
            r0 = pl.multiple_of(r * CONV_ROWS, CONV_ROWS)
            acc = jnp.broadcast_to(cb_ref[...], (CONV_ROWS, c_ref.shape[1]))
            for k in range(n_taps):
                acc = acc + glu_full[pl.ds(r0 + k * bb, CONV_ROWS), :] * cw_ref[k:k + 1, :]
            c_ref[pl.ds(r0, CONV_ROWS), :] = acc
            return carry

        lax.fori_loop(0, n_blocks, block, 0)
        return
    for lane0 in range(0, c_ref.shape[1], LANES):
        lanes = slice(lane0, lane0 + LANES)
        taps = [jnp.broadcast_to(cw_ref[k:k + 1, lanes], (SUBLANES, LANES)) for k in range(n_taps)]
        bias = jnp.broadcast_to(cb_ref[:, lanes], (SUBLANES, LANES))

        def block(r, carry, lanes=lanes, taps=taps, bias=bias):
            r0 = pl.multiple_of(r * CONV_ROWS, CONV_ROWS)
            _conv_tile_dense(glu_full, c_ref, r0, lanes, taps, bias)
            return carry

        lax.fori_loop(0, n_blocks, block, 0)


def _mixer_ab_kernel(tt, bb, has_state, start_pos, *refs):
    x_ref, refs = refs[0], refs[1:]
    if has_state:
        sp_ref, sc_ref = refs[:2]
    else:
        p_ref, pt_ref = refs[:2]
    (g_ref, win_ref, pw_ref, ps_ref, cw_ref, cb_ref, lg_ref, lb_ref, wout_ref,
     xo_ref, po_ref, co_ref, xa_full, glu_full, c_ref) = refs[2:]
    rows = tt * bb
    hp = POOL_BUF * bb
    hc = (cw_ref.shape[0] - 1) * bb
    ti = pl.program_id(1)

    if has_state:
        xa_full[0:hp, :] = sp_ref[...].reshape(hp, xa_full.shape[1])
        glu_full[0:hc, :] = sc_ref[...].reshape(hc, glu_full.shape[1])
    else:
        @pl.when(ti == 0)
        def _():
            xa_full[...] = jnp.zeros(xa_full.shape, F32)
            glu_full[...] = jnp.zeros(glu_full.shape, F32)

        xa_full[0:hp, :] = xa_full[rows:rows + hp, :]
        glu_full[0:hc, :] = glu_full[rows:rows + hc, :]

    x = x_ref[...].reshape(rows, x_ref.shape[2])
    h = _rmsnorm(x, g_ref[...]).astype(BF16)
    if not has_state:
        h = _bdot(p_ref[...], h).astype(BF16)
    glu, ya = _ab_front(h, win_ref, pw_ref, ps_ref, xa_full, bb, start_pos + ti * tt)
    glu_full[hc:hc + rows, :] = glu
    po_ref[...] = xa_full[rows:rows + hp, :].reshape(po_ref.shape)
    co_ref[...] = glu_full[rows:rows + hc, :].reshape(co_ref.shape)

    _depthwise_conv(glu_full, c_ref, rows, bb, cw_ref, cb_ref)
    yb = _silu(_layernorm(c_ref[...], lg_ref[...], lb_ref[...])).astype(BF16)
    y = jnp.concatenate([ya, yb], axis=1)
    if not has_state:
        y = _bdot(pt_ref[...], y).astype(BF16)
    out = x + _bdot(y, wout_ref[...])
    xo_ref[...] = out.reshape(xo_ref.shape)


def _ffn_conv_slabs(full_ref, up, halo_rows, fw, fb, bb, rows):
    halo = halo_rows.shape[0]
    full_ref[0:halo, :] = halo_rows
    full_ref[halo:halo + rows, :] = up
    tail = full_ref[rows:rows + halo, :]
    cv = (full_ref[0:rows, :] * fw[0:1, :] + full_ref[bb:bb + rows, :] * fw[1:2, :]
          + up * fw[2:3, :] + fb)
    return cv, tail


def _ffn_conv_rolled(up, prev8, fw, fb):
    rows, width = up.shape
    groups = up.reshape(rows // SUBLANES, SUBLANES, width)
    ext = jnp.concatenate([prev8[None], groups], axis=0)
    sub = lax.broadcasted_iota(jnp.int32, (1, SUBLANES, width), 1)
    cv = groups * fw[2:3, :] + fb
    for shift in (1, 2):
        rot = pltpu.roll(ext, shift, 1)
        cv = cv + jnp.where(sub < shift, rot[:-1], rot[1:]) * fw[2 - shift:3 - shift, :]
    return cv.reshape(rows, width)


def _ffn_kernel(tt, bb, has_state, final_norm, *refs):
    x_ref, refs = refs[0], refs[1:]
    if has_state:
        st_ref, refs = refs[0], refs[1:]
    (g_ref, wup_ref, fw_ref, fb_ref, wdown_ref, fin_ref,
     xo_ref, so_ref, h_ref, acc_ref, full_a_ref, full_g_ref, carry_ref) = refs
    rows = tt * bb
    d_ff = wdown_ref.shape[0]
    n_keep = fw_ref.shape[0] - 1
    ti = pl.program_id(1)

    d_model = x_ref.shape[2]
    h_ref[...] = _rmsnorm(x_ref[...].reshape(rows, d_model), g_ref[...]).astype(BF16)

    if not has_state:
        @pl.when(ti == 0)
        def _():
            carry_ref[...] = jnp.zeros(carry_ref.shape, F32)

    def up_pair(c):
        return [_bdot(h_ref[...], wup_ref[:, off:off + FF_CHUNK])
                for off in (c * FF_CHUNK, d_ff + c * FF_CHUNK)]

    n_chunks = d_ff // FF_CHUNK
    ups_ahead = [up_pair(c) for c in range(UP_AHEAD)]
    acts = []
    for c in range(n_chunks):
        ups = ups_ahead.pop(0)
        if c + UP_AHEAD < n_chunks:
            ups_ahead.append(up_pair(c + UP_AHEAD))
        col_pair = [slice(off, off + FF_CHUNK) for off in (c * FF_CHUNK, d_ff + c * FF_CHUNK)]
        conv = []
        for full_ref, cols, up in zip((full_a_ref, full_g_ref), col_pair, ups):
            if bb == 1:
                conv.append(_ffn_conv_rolled(up, carry_ref[:, cols], fw_ref[:, cols], fb_ref[:, cols]))
                carry_ref[:, cols] = up[rows - SUBLANES:, :]
                so_ref[0, :, cols] = up[rows - n_keep:, :]
            else:
                if has_state:
                    halo_rows = st_ref[:, :, cols].reshape(n_keep * bb, FF_CHUNK)
                else:
                    halo_rows = carry_ref[:, cols]
                cv, tail = _ffn_conv_slabs(full_ref, up, halo_rows, fw_ref[:, cols], fb_ref[:, cols],
                                           bb, rows)
                so_ref[:, :, cols] = tail.reshape(n_keep, bb, FF_CHUNK)
                if not has_state:
                    carry_ref[:, cols] = tail
                conv.append(cv)
        acts.append((_silu(conv[0]) * conv[1]).astype(BF16))
        if len(acts) == DOWN_GROUP or c + 1 == n_chunks:
            lo = (c + 1 - len(acts)) * FF_CHUNK
            act = acts[0] if len(acts) == 1 else jnp.concatenate(acts, axis=1)
            base = x_ref[...].reshape(rows, d_model) if lo == 0 else acc_ref[...]
            total = base + _bdot(act, wdown_ref[lo:(c + 1) * FF_CHUNK, :])
            if c + 1 < n_chunks:
                acc_ref[...] = total
            else:
                if final_norm:
                    total = _rmsnorm(total, fin_ref[...])
                xo_ref[...] = total.reshape(xo_ref.shape)
            acts = []


def _mixer_c_kernel(tt, bb, *refs):
    (x_ref, g_ref, win_ref, lg_ref, lb_ref, sw_ref, sb_ref, wout_ref,
     xo_ref, vo_ref, v_ref, vb_ref, z_ref) = refs
    rows = tt * bb
    d_c = wout_ref.shape[0]

    x = x_ref[...].reshape(rows, x_ref.shape[2])
    h = _rmsnorm(x, g_ref[...]).astype(BF16)
    v = _layernorm(_bdot(h, win_ref[:, d_c:]), lg_ref[...], lb_ref[...])
    u = _bdot(h, win_ref[:, :d_c])

    if bb == 1:
        n_heads = sw_ref.shape[0]
        dh = d_c // n_heads
        n_chunks = tt // CHUNK
        vo_ref[0] = v[rows - CHUNK:, :]
        vb_ref[...] = v.astype(BF16)
        row_i = lax.broadcasted_iota(jnp.int32, (CHUNK, CHUNK), 0)
        col_i = lax.broadcasted_iota(jnp.int32, (CHUNK, CHUNK), 1)
        for hd in range(n_heads):
            hcols = slice(hd * dh, (hd + 1) * dh)
            m = jnp.where(col_i <= row_i, sw_ref[hd], 0.0).astype(BF16)
            rhs = jnp.concatenate(
                [vb_ref[c * CHUNK:(c + 1) * CHUNK, hcols] for c in range(n_chunks)], axis=1)
            zc = _bdot(m, rhs)
            bias = sb_ref[:, hcols]
            for c in range(n_chunks):
                z_ref[c * CHUNK:(c + 1) * CHUNK, hcols] = zc[:, c * dh:(c + 1) * dh] + bias
    else:
        vo_ref[...] = v.reshape(vo_ref.shape)
        v_ref[...] = v
        for i in range(tt):
            zi = jnp.broadcast_to(sb_ref[i:i + 1, :], (bb, d_c))
            for j in range(i + 1):
                zi = zi + v_ref[j * bb:(j + 1) * bb, :] * sw_ref[i * tt + j:i * tt + j + 1, :]
            z_ref[i * bb:(i + 1) * bb, :] = zi

    gated = (u * z_ref[...]).astype(BF16)
    out = x + _bdot(gated, wout_ref[...])
    xo_ref[...] = out.reshape(xo_ref.shape)


def _layer_shape(entry):
    return entry[0].shape[1:] if isinstance(entry, tuple) else entry.shape


def _resident(entries):
    arrays, specs = [], []
    for entry in entries:
        arr, layer = entry if isinstance(entry, tuple) else (entry, None)
        if layer is None:
            block, index = arr.shape, (0,) * arr.ndim
        else:
            block, index = (None,) + arr.shape[1:], (layer,) + (0,) * (arr.ndim - 1)
        arrays.append(arr)
        specs.append(pl.BlockSpec(block, lambda bi, ti, index=index: index,
                                  pipeline_mode=pl.Buffered(1)))
    return arrays, specs


def _tm_spec(tt, bb, width):
    return pl.BlockSpec((tt, bb, width), lambda bi, ti: (ti, bi, 0))


def _bm_spec(bb, tt, width):
    return pl.BlockSpec((bb, tt, width), lambda bi, ti: (bi, ti, 0))


def _tm_state_spec(n, bb, width):
    return pl.BlockSpec((n, bb, width), lambda bi, ti: (0, bi, 0))


_PARAMS = pltpu.CompilerParams(
    dimension_semantics=("arbitrary", "arbitrary"), vmem_limit_bytes=VMEM_LIMIT)


def _row_permutation(tt, bb):
    m = np.arange(tt * bb)
    src = (m % bb) * tt + m // bb
    return jnp.asarray(src[:, None] == m[None, :], BF16)


def _ab_weights(w):
    return (w["mix_norm"], w["w_in_ab"], w["pool_w"], w["pool_scale"], w["conv_w"], w["conv_b"],
            w["conv_norm_g"], w["conv_norm_b"], w["w_out_ab"])


def _mixer_ab(x, states, w, tt, bb, start_pos, name):
    has_state = states is not None
    if has_state:
        t_len, batch, d_model = x.shape
        x_spec = _tm_spec(tt, bb, d_model)
    else:
        batch, t_len, d_model = x.shape
        x_spec = _bm_spec(bb, tt, d_model)
        assert bb == batch and bb % SUBLANES == 0
    d_a = _layer_shape(w["pool_scale"])[1]
    d_b = _layer_shape(w["conv_b"])[1]
    conv_k = _layer_shape(w["conv_w"])[0]
    assert t_len == tt or tt >= conv_k - 1
    rows = tt * bb
    if has_state:
        extra = list(states)
        extra_specs = [_tm_state_spec(POOL_BUF, bb, d_a), _tm_state_spec(conv_k - 1, bb, d_b)]
    else:
        perm = _row_permutation(tt, bb)
        extra, extra_specs = _resident([perm, perm.T])
    weights, weight_specs = _resident(_ab_weights(w))
    return pl.pallas_call(
        functools.partial(_mixer_ab_kernel, tt, bb, has_state, start_pos),
        grid=(batch // bb, t_len // tt),
        in_specs=[x_spec] + extra_specs + weight_specs,
        out_specs=[x_spec, _tm_state_spec(POOL_BUF, bb, d_a), _tm_state_spec(conv_k - 1, bb, d_b)],
        out_shape=[jax.ShapeDtypeStruct(x.shape, F32),
                   jax.ShapeDtypeStruct((POOL_BUF, batch, d_a), F32),
                   jax.ShapeDtypeStruct((conv_k - 1, batch, d_b), F32)],
        scratch_shapes=[pltpu.VMEM(((tt + POOL_BUF) * bb, d_a), F32),
                        pltpu.VMEM(((tt + conv_k - 1) * bb, d_b), F32),
                        pltpu.VMEM((rows, d_b), F32)],
        compiler_params=_PARAMS,
        name=name,
    )(x, *extra, *weights)


def _ffn(x, state, w, tt, bb, final_norm, name):
    d_ff = _layer_shape(w["w_down"])[0]
    n_keep = _layer_shape(w["ffn_conv_w"])[0] - 1
    has_state = state is not None
    if bb == 1:
        assert not has_state and tt % SUBLANES == 0
        batch, t_len, d_model = x.shape
        x_spec = _bm_spec(1, tt, d_model)
        st_spec = pl.BlockSpec((1, n_keep, 2 * d_ff), lambda bi, ti: (bi, 0, 0))
        st_shape = (batch, n_keep, 2 * d_ff)
        carry_rows = SUBLANES
    else:
        t_len, batch, d_model = x.shape
        x_spec = _tm_spec(tt, bb, d_model)
        st_spec = _tm_state_spec(n_keep, bb, 2 * d_ff)
        st_shape = (n_keep, batch, 2 * d_ff)
        carry_rows = n_keep * bb
    rows = tt * bb
    weights, weight_specs = _resident(
        (w["ffn_norm"], w["w_up"], w["ffn_conv_w"], w["ffn_conv_b"], w["w_down"], w["final_norm"]))
    in_specs = [x_spec] + ([st_spec] if has_state else []) + weight_specs
    return pl.pallas_call(
        functools.partial(_ffn_kernel, tt, bb, has_state, final_norm),
        grid=(batch // bb, t_len // tt),
        in_specs=in_specs,
        out_specs=[x_spec, st_spec],
        out_shape=[jax.ShapeDtypeStruct(x.shape, F32), jax.ShapeDtypeStruct(st_shape, F32)],
        scratch_shapes=[pltpu.VMEM((rows, d_model), BF16),
                        pltpu.VMEM((rows, d_model), F32),
                        pltpu.VMEM((rows + carry_rows, FF_CHUNK), F32),
                        pltpu.VMEM((rows + carry_rows, FF_CHUNK), F32),
                        pltpu.VMEM((carry_rows, 2 * d_ff), F32)],
        compiler_params=_PARAMS,
        name=name,
    )(x, *((state,) if has_state else ()), *weights)


def _mixer_c(x, w, tt, bb, name):
    d_c = _layer_shape(w["w_out_c"])[0]
    if bb == 1:
        assert tt % CHUNK == 0
        batch, t_len, d_model = x.shape
        x_spec = _bm_spec(1, tt, d_model)
        v_spec = pl.BlockSpec((1, CHUNK, d_c), lambda bi, ti: (bi, 0, 0))
        v_shape = (batch, CHUNK, d_c)
        sgu = (w["sgu_w"], w["sgu_b_chunk"])
    else:
        t_len, batch, d_model = x.shape
        assert t_len == tt
        x_spec = _tm_spec(tt, bb, d_model)
        v_spec = _tm_spec(tt, bb, d_c)
        v_shape = (t_len, batch, d_c)
        sgu = (w["sgu_w_short"], w["sgu_b_short"])
    rows = tt * bb
    weights, weight_specs = _resident(
        (w["mix_norm"], w["w_in_c"], w["sgu_norm_g"], w["sgu_norm_b"]) + sgu + (w["w_out_c"],))
    return pl.pallas_call(
        functools.partial(_mixer_c_kernel, tt, bb),
        grid=(batch // bb, t_len // tt),
        in_specs=[x_spec] + weight_specs,
        out_specs=[x_spec, v_spec],
        out_shape=[jax.ShapeDtypeStruct(x.shape, F32), jax.ShapeDtypeStruct(v_shape, F32)],
        scratch_shapes=[pltpu.VMEM((rows, d_c), F32),
                        pltpu.VMEM((rows, d_c), BF16),
                        pltpu.VMEM((rows, d_c), F32)],
        compiler_params=_PARAMS,
        name=name,
    )(x, *weights)


def _tm(a):
    return jnp.swapaxes(a, 0, 1)


def kernel(x_prompt, x_sample, state_pool, state_conv, state_ffn, mix_norm, w_in_ab, pool_w, pool_scale, conv_w, conv_b, conv_norm_g, conv_norm_b, w_out_ab, w_in_c, sgu_norm_g, sgu_norm_b, sgu_w, sgu_bias, w_out_c, ffn_norm, w_up, ffn_conv_w, ffn_conv_b, w_down, final_norm):
    batch_p = x_prompt.shape[0]
    dec_seq = x_sample.shape[1]
    n_heads = sgu_w.shape[1]
    d_c = w_out_c.shape[1]
    dh = d_c // n_heads
    row = lambda a: a.reshape(1, -1)
    rows_of = lambda a: a.reshape(a.shape[0], 1, -1)
    fin = row(final_norm)
    mix_n, ffn_n, ffn_cb = rows_of(mix_norm), rows_of(ffn_norm), rows_of(ffn_conv_b)
    w_up_b, w_down_b = w_up.astype(BF16), w_down.astype(BF16)
    ffn = lambda layer: dict(ffn_norm=(ffn_n, layer), w_up=(w_up_b, layer), ffn_conv_w=(ffn_conv_w, layer),
                             ffn_conv_b=(ffn_cb, layer), w_down=(w_down_b, layer), final_norm=fin)
    l0 = dict(mix_norm=(mix_n, 0), w_in_ab=w_in_ab[0].astype(BF16), pool_w=pool_w[0].astype(BF16),
              pool_scale=row(pool_scale[0]), conv_w=conv_w[0], conv_b=row(conv_b[0]),
              conv_norm_g=row(conv_norm_g[0]), conv_norm_b=row(conv_norm_b[0]),
              w_out_ab=w_out_ab[0].astype(BF16), **ffn(0))
    sw_short = jnp.repeat(jnp.transpose(sgu_w[0][:, :dec_seq, :dec_seq], (1, 2, 0)), dh, axis=2)
    l1 = dict(mix_norm=(mix_n, 1), w_in_c=w_in_c[0].astype(BF16),
              sgu_norm_g=row(sgu_norm_g[0]), sgu_norm_b=row(sgu_norm_b[0]),
              sgu_w=sgu_w[0], sgu_b_chunk=jnp.repeat(sgu_bias[0].T, dh, axis=1),
              sgu_w_short=sw_short.reshape(dec_seq * dec_seq, d_c),
              sgu_b_short=jnp.repeat(sgu_bias[0][:, :dec_seq].T, dh, axis=1),
              w_out_c=w_out_c[0].astype(BF16), **ffn(1))

    xp, pool_p, conv_p = _mixer_ab(x_prompt, None, l0, 64, batch_p, 0, "mixer_ab_prompt")
    xp, f0_p = _ffn(xp, None, l0, 512, 1, False, "ffn0_prompt")
    xp, v_p = _mixer_c(xp, l1, 512, 1, "mixer_c_prompt")
    yp, f1_p = _ffn(xp, None, l1, 512, 1, True, "ffn1_prompt")

    bb_s = 64
    xs, pool_s, conv_s = _mixer_ab(_tm(x_sample), (_tm(state_pool[0]), _tm(state_conv[0])), l0,
                                   dec_seq, bb_s, PAST_LEN, "mixer_ab_sample")
    xs, f0_s = _ffn(xs, _tm(state_ffn[0]), l0, dec_seq, bb_s, False, "ffn0_sample")
    xs, v_s = _mixer_c(xs, l1, dec_seq, bb_s, "mixer_c_sample")
    ys, f1_s = _ffn(xs, _tm(state_ffn[1]), l1, dec_seq, bb_s, True, "ffn1_sample")

    return (yp, _tm(ys), _tm(pool_p)[None], _tm(pool_s)[None], _tm(conv_p)[None], _tm(conv_s)[None],
            v_p[None], _tm(v_s)[None],
            jnp.stack([f0_p, f1_p]), jnp.stack([_tm(f0_s), _tm(f1_s)]))
```

```python
import functools

import jax
import jax.numpy as jnp
from jax import lax
from jax.experimental import pallas as pl
from jax.experimental.pallas import tpu as pltpu

F32 = jnp.float32
BF16 = jnp.bfloat16

EPS = 1e-6
POOL_WINDOWS = (2, 4, 8, 16)
POOL_BUF = max(POOL_WINDOWS) - 1
CHUNK = 128
PAST_LEN = 16384
LANES = 128
SUBLANES = 8
FF_CHUNK = 256
DOWN_GROUP = 4
UP_AHEAD = 3
CONV_ROWS = 64
VMEM_LIMIT = 56 * 1024 * 1024


def _rmsnorm(x, g):
    ms = jnp.mean(x * x, axis=-1, keepdims=True)
    return x * lax.rsqrt(ms + EPS) * g


def _layernorm(x, g, b):
    mu = jnp.mean(x, axis=-1, keepdims=True)
    xc = x - mu
    var = jnp.mean(xc * xc, axis=-1, keepdims=True)
    return xc * lax.rsqrt(var + EPS) * g + b


def _silu(x):
    return x * jax.nn.sigmoid(x)


def _bdot(a, b):
    return jnp.dot(a, b, preferred_element_type=F32)


def _ab_front(h, win_ref, pw_ref, ps_ref, xa_full, bb, pos0):
    rows = h.shape[0]
    d_a = xa_full.shape[1]
    hp = POOL_BUF * bb
    proj = _bdot(h, win_ref[...])
    d_b = (proj.shape[1] - d_a) // 2
    xa = proj[:, :d_a]
    glu = proj[:, d_a:d_a + d_b] * jax.nn.sigmoid(proj[:, d_a + d_b:])
    xa_full[hp:hp + rows, :] = xa
    pos = pos0 + lax.broadcasted_iota(jnp.int32, (rows, LANES), 0) // bb
    group = d_a // len(POOL_WINDOWS)
    ya = []
    for gi, w in enumerate(POOL_WINDOWS):
        c0 = gi * group
        xg = xa[:, c0:c0 + group]
        win = xg
        for k in range(1, w):
            win = win + xa_full[hp - k * bb:hp - k * bb + rows, c0:c0 + group]
        cnt = jnp.minimum(pos + 1, w).astype(F32)
        d = (win / cnt - xg).astype(BF16)
        ya.append((_bdot(d, pw_ref[gi]) * ps_ref[:, c0:c0 + group]).astype(BF16))
    return glu, jnp.concatenate(ya, axis=1)


def _conv_tile_dense(glu_full, c_ref, r0, lanes, taps, bias):
    n_out = CONV_ROWS // SUBLANES
    acc = [bias] * n_out
    for m in range(n_out + len(taps) - 1):
        grp = glu_full[pl.ds(r0 + m * SUBLANES, SUBLANES), lanes]
        for k in range(len(taps)):
            if 0 <= m - k < n_out:
                acc[m - k] = acc[m - k] + grp * taps[k]
    for i in range(n_out):
        c_ref[pl.ds(r0 + i * SUBLANES, SUBLANES), lanes] = acc[i]


def _depthwise_conv(glu_full, c_ref, rows, bb, cw_ref, cb_ref):
    n_taps = cw_ref.shape[0]
    n_blocks = rows // CONV_ROWS
    if bb != SUBLANES:
        def block(r, carry):
            r0 = pl.multiple_of(r * CONV_ROWS, CONV_ROWS)
            acc = jnp.broadcast_to(cb_ref[...], (CONV_ROWS, c_ref.shape[1]))
            for k in range(n_taps):
                acc = acc + glu_full[pl.ds(r0 + k * bb, CONV_ROWS), :] * cw_ref[k:k + 1, :]
            c_ref[pl.ds(r0, CONV_ROWS), :] = acc
            return carry

        lax.fori_loop(0, n_blocks, block, 0)
        return
    for lane0 in range(0, c_ref.shape[1], LANES):
        lanes = slice(lane0, lane0 + LANES)
        taps = [jnp.broadcast_to(cw_ref[k:k + 1, lanes], (SUBLANES, LANES)) for k in range(n_taps)]
        bias = jnp.broadcast_to(cb_ref[:, lanes], (SUBLANES, LANES))

        def block(r, carry, lanes=lanes, taps=taps, bias=bias):
            r0 = pl.multiple_of(r * CONV_ROWS, CONV_ROWS)
            _conv_tile_dense(glu_full, c_ref, r0, lanes, taps, bias)
            return carry

        lax.fori_loop(0, n_blocks, block, 0)


def _mixer_ab_kernel(tt, bb, has_state, start_pos, *refs):
    x_ref, refs = refs[0], refs[1:]
    if has_state:
        (sp_ref, sc_ref), refs = refs[:2], refs[2:]
    (g_ref, win_ref, pw_ref, ps_ref, cw_ref, cb_ref, lg_ref, lb_ref, wout_ref,
     xo_ref, po_ref, co_ref, xa_full, glu_full, c_ref) = refs
    rows = tt * bb
    hp = POOL_BUF * bb
    hc = (cw_ref.shape[0] - 1) * bb
    ti = pl.program_id(1)

    if has_state:
        xa_full[0:hp, :] = sp_ref[...].reshape(hp, xa_full.shape[1])
        glu_full[0:hc, :] = sc_ref[...].reshape(hc, glu_full.shape[1])
    else:
        @pl.when(ti == 0)
        def _():
            xa_full[...] = jnp.zeros(xa_full.shape, F32)
            glu_full[...] = jnp.zeros(glu_full.shape, F32)

        xa_full[0:hp, :] = xa_full[rows:rows + hp, :]
        glu_full[0:hc, :] = glu_full[rows:rows + hc, :]

    x = x_ref[...].reshape(rows, x_ref.shape[2])
    h = _rmsnorm(x, g_ref[...]).astype(BF16)
    glu, ya = _ab_front(h, win_ref, pw_ref, ps_ref, xa_full, bb, start_pos + ti * tt)
    glu_full[hc:hc + rows, :] = glu
    po_ref[...] = xa_full[rows:rows + hp, :].reshape(po_ref.shape)
    co_ref[...] = glu_full[rows:rows + hc, :].reshape(co_ref.shape)

    _depthwise_conv(glu_full, c_ref, rows, bb, cw_ref, cb_ref)
    yb = _silu(_layernorm(c_ref[...], lg_ref[...], lb_ref[...])).astype(BF16)
    out = x + _bdot(jnp.concatenate([ya, yb], axis=1), wout_ref[...])
    xo_ref[...] = out.reshape(xo_ref.shape)


def _ffn_conv_slabs(full_ref, up, halo_rows, fw, fb, bb, rows):
    halo = halo_rows.shape[0]
    full_ref[0:halo, :] = halo_rows
    full_ref[halo:halo + rows, :] = up
    tail = full_ref[rows:rows + halo, :]
    cv = (full_ref[0:rows, :] * fw[0:1, :] + full_ref[bb:bb + rows, :] * fw[1:2, :]
          + up * fw[2:3, :] + fb)
    return cv, tail


def _ffn_kernel(tt, bb, has_state, final_norm, *refs):
    x_ref, refs = refs[0], refs[1:]
    if has_state:
        st_ref, refs = refs[0], refs[1:]
    (g_ref, wup_ref, fw_ref, fb_ref, wdown_ref, fin_ref,
     xo_ref, so_ref, h_ref, acc_ref, full_a_ref, full_g_ref, carry_ref) = refs
    rows = tt * bb
    d_ff = wdown_ref.shape[0]
    n_keep = fw_ref.shape[0] - 1
    ti = pl.program_id(1)

    d_model = x_ref.shape[2]
    h_ref[...] = _rmsnorm(x_ref[...].reshape(rows, d_model), g_ref[...]).astype(BF16)

    if not has_state:
        @pl.when(ti == 0)
        def _():
            carry_ref[...] = jnp.zeros(carry_ref.shape, F32)

    def up_pair(c):
        return [_bdot(h_ref[...], wup_ref[:, off:off + FF_CHUNK])
                for off in (c * FF_CHUNK, d_ff + c * FF_CHUNK)]

    n_chunks = d_ff // FF_CHUNK
    ups_ahead = [up_pair(c) for c in range(UP_AHEAD)]
    acts = []
    for c in range(n_chunks):
        ups = ups_ahead.pop(0)
        if c + UP_AHEAD < n_chunks:
            ups_ahead.append(up_pair(c + UP_AHEAD))
        col_pair = [slice(off, off + FF_CHUNK) for off in (c * FF_CHUNK, d_ff + c * FF_CHUNK)]
        conv = []
        for full_ref, cols, up in zip((full_a_ref, full_g_ref), col_pair, ups):
            if has_state:
                halo_rows = st_ref[:, :, cols].reshape(n_keep * bb, FF_CHUNK)
            else:
                halo_rows = carry_ref[:, cols]
            cv, tail = _ffn_conv_slabs(full_ref, up, halo_rows, fw_ref[:, cols], fb_ref[:, cols], bb, rows)
            so_ref[:, :, cols] = tail.reshape(n_keep, bb, FF_CHUNK)
            if not has_state:
                carry_ref[:, cols] = tail
            conv.append(cv)
        acts.append((_silu(conv[0]) * conv[1]).astype(BF16))
        if len(acts) == DOWN_GROUP or c + 1 == n_chunks:
            lo = (c + 1 - len(acts)) * FF_CHUNK
            act = acts[0] if len(acts) == 1 else jnp.concatenate(acts, axis=1)
            base = x_ref[...].reshape(rows, d_model) if lo == 0 else acc_ref[...]
            total = base + _bdot(act, wdown_ref[lo:(c + 1) * FF_CHUNK, :])
            if c + 1 < n_chunks:
                acc_ref[...] = total
            else:
                if final_norm:
                    total = _rmsnorm(total, fin_ref[...])
                xo_ref[...] = total.reshape(xo_ref.shape)
            acts = []


def _tile_copies(hbm, buf, sems, slot, step, tt, to_hbm):
    copies = []
    for b in range(buf.shape[2]):
        rows_hbm = hbm.at[b, pl.ds(step * tt, tt), :]
        rows_vmem = buf.at[slot, :, b, :]
        src, dst = (rows_vmem, rows_hbm) if to_hbm else (rows_hbm, rows_vmem)
        copies.append(pltpu.make_async_copy(src, dst, sems.at[slot, b]))
    return copies


def _with_relayout(tt, x_hbm, xo_hbm, xbuf, obuf, in_sem, out_sem, compute):
    s = pl.program_id(1)
    n_steps = pl.num_programs(1)
    slot = s % 2

    def start(copies):
        for c in copies:
            c.start()

    def wait(copies):
        for c in copies:
            c.wait()

    @pl.when(s == 0)
    def _():
        start(_tile_copies(x_hbm, xbuf, in_sem, 0, 0, tt, False))

    @pl.when(s + 1 < n_steps)
    def _():
        start(_tile_copies(x_hbm, xbuf, in_sem, 1 - slot, s + 1, tt, False))

    wait(_tile_copies(x_hbm, xbuf, in_sem, slot, s, tt, False))

    @pl.when(s >= 2)
    def _():
        wait(_tile_copies(xo_hbm, obuf, out_sem, slot, s - 2, tt, True))

    compute(xbuf.at[slot], obuf.at[slot])
    start(_tile_copies(xo_hbm, obuf, out_sem, slot, s, tt, True))

    @pl.when(s == n_steps - 1)
    def _():
        @pl.when(s >= 1)
        def _():
            wait(_tile_copies(xo_hbm, obuf, out_sem, 1 - slot, s - 1, tt, True))
        wait(_tile_copies(xo_hbm, obuf, out_sem, slot, s, tt, True))


def _ffn_relayout_kernel(tt, bb, final_norm, x_hbm, *refs):
    weights, (xo_hbm, so_ref, xbuf, obuf, in_sem, out_sem), scratch = refs[:6], refs[6:12], refs[12:]
    _with_relayout(tt, x_hbm, xo_hbm, xbuf, obuf, in_sem, out_sem,
                   lambda x_tile, out_tile: _ffn_kernel(tt, bb, False, final_norm, x_tile, *weights,
                                                        out_tile, so_ref, *scratch))


def _mixer_ab_relayout_kernel(tt, bb, x_hbm, *refs):
    weights, (xo_hbm, po_ref, co_ref, xbuf, obuf, in_sem, out_sem), scratch = refs[:9], refs[9:16], refs[16:]
    _with_relayout(tt, x_hbm, xo_hbm, xbuf, obuf, in_sem, out_sem,
                   lambda x_tile, out_tile: _mixer_ab_kernel(tt, bb, False, 0, x_tile, *weights,
                                                             out_tile, po_ref, co_ref, *scratch))


def _mixer_c_kernel(tt, bb, *refs):
    (x_ref, g_ref, win_ref, lg_ref, lb_ref, sw_ref, sb_ref, wout_ref,
     xo_ref, vo_ref, v_ref, vb_ref, z_ref) = refs
    rows = tt * bb
    d_c = wout_ref.shape[0]

    x = x_ref[...].reshape(rows, x_ref.shape[2])
    h = _rmsnorm(x, g_ref[...]).astype(BF16)
    v = _layernorm(_bdot(h, win_ref[:, d_c:]), lg_ref[...], lb_ref[...])
    u = _bdot(h, win_ref[:, :d_c])

    if bb == 1:
        n_heads = sw_ref.shape[0]
        dh = d_c // n_heads
        n_chunks = tt // CHUNK
        vo_ref[0] = v[rows - CHUNK:, :]
        vb_ref[...] = v.astype(BF16)
        row_i = lax.broadcasted_iota(jnp.int32, (CHUNK, CHUNK), 0)
        col_i = lax.broadcasted_iota(jnp.int32, (CHUNK, CHUNK), 1)
        for hd in range(n_heads):
            hcols = slice(hd * dh, (hd + 1) * dh)
            m = jnp.where(col_i <= row_i, sw_ref[hd], 0.0).astype(BF16)
            rhs = jnp.concatenate(
                [vb_ref[c * CHUNK:(c + 1) * CHUNK, hcols] for c in range(n_chunks)], axis=1)
            zc = _bdot(m, rhs)
            bias = sb_ref[:, hcols]
            for c in range(n_chunks):
                z_ref[c * CHUNK:(c + 1) * CHUNK, hcols] = zc[:, c * dh:(c + 1) * dh] + bias
    else:
        vo_ref[...] = v.reshape(vo_ref.shape)
        v_ref[...] = v
        for i in range(tt):
            zi = jnp.broadcast_to(sb_ref[i:i + 1, :], (bb, d_c))
            for j in range(i + 1):
                zi = zi + v_ref[j * bb:(j + 1) * bb, :] * sw_ref[i * tt + j:i * tt + j + 1, :]
            z_ref[i * bb:(i + 1) * bb, :] = zi

    gated = (u * z_ref[...]).astype(BF16)
    out = x + _bdot(gated, wout_ref[...])
    xo_ref[...] = out.reshape(xo_ref.shape)


def _layer_shape(entry):
    return entry[0].shape[1:] if isinstance(entry, tuple) else entry.shape


def _resident(entries):
    arrays, specs = [], []
    for entry in entries:
        arr, layer = entry if isinstance(entry, tuple) else (entry, None)
        if layer is None:
            block, index = arr.shape, (0,) * arr.ndim
        else:
            block, index = (None,) + arr.shape[1:], (layer,) + (0,) * (arr.ndim - 1)
        arrays.append(arr)
        specs.append(pl.BlockSpec(block, lambda bi, ti, index=index: index,
                                  pipeline_mode=pl.Buffered(1)))
    return arrays, specs


def _tm_spec(tt, bb, width):
    return pl.BlockSpec((tt, bb, width), lambda bi, ti: (ti, bi, 0))


def _bm_spec(bb, tt, width):
    return pl.BlockSpec((bb, tt, width), lambda bi, ti: (bi, ti, 0))


def _tm_state_spec(n, bb, width):
    return pl.BlockSpec((n, bb, width), lambda bi, ti: (0, bi, 0))


_PARAMS = pltpu.CompilerParams(
    dimension_semantics=("arbitrary", "arbitrary"), vmem_limit_bytes=VMEM_LIMIT)


def _ab_weights(w):
    return (w["mix_norm"], w["w_in_ab"], w["pool_w"], w["pool_scale"], w["conv_w"], w["conv_b"],
            w["conv_norm_g"], w["conv_norm_b"], w["w_out_ab"])


def _ab_dims(w):
    return (_layer_shape(w["pool_scale"])[1], _layer_shape(w["conv_b"])[1], _layer_shape(w["conv_w"])[0])


def _ab_scratch(tt, bb, d_a, d_b, conv_k):
    return [pltpu.VMEM(((tt + POOL_BUF) * bb, d_a), F32),
            pltpu.VMEM(((tt + conv_k - 1) * bb, d_b), F32),
            pltpu.VMEM((tt * bb, d_b), F32)]


def _relayout_scratch(tt, batch, d_model):
    return [pltpu.VMEM((2, tt, batch, d_model), F32), pltpu.VMEM((2, tt, batch, d_model), F32),
            pltpu.SemaphoreType.DMA((2, batch)), pltpu.SemaphoreType.DMA((2, batch))]


def _mixer_ab_stateful(x, states, w, bb, start_pos, name):
    tt, batch, d_model = x.shape
    d_a, d_b, conv_k = _ab_dims(w)
    state_specs = [_tm_state_spec(POOL_BUF, bb, d_a), _tm_state_spec(conv_k - 1, bb, d_b)]
    weights, weight_specs = _resident(_ab_weights(w))
    return pl.pallas_call(
        functools.partial(_mixer_ab_kernel, tt, bb, True, start_pos),
        grid=(batch // bb, 1),
        in_specs=[_tm_spec(tt, bb, d_model)] + state_specs + weight_specs,
        out_specs=[_tm_spec(tt, bb, d_model)] + state_specs,
        out_shape=[jax.ShapeDtypeStruct(x.shape, F32),
                   jax.ShapeDtypeStruct((POOL_BUF, batch, d_a), F32),
                   jax.ShapeDtypeStruct((conv_k - 1, batch, d_b), F32)],
        scratch_shapes=_ab_scratch(tt, bb, d_a, d_b, conv_k),
        compiler_params=_PARAMS,
        name=name,
    )(x, *states, *weights)


def _mixer_ab_relayout(x, w, tt, name):
    batch, t_len, d_model = x.shape
    d_a, d_b, conv_k = _ab_dims(w)
    assert batch == SUBLANES and t_len % tt == 0 and tt >= conv_k - 1
    weights, weight_specs = _resident(_ab_weights(w))
    return pl.pallas_call(
        functools.partial(_mixer_ab_relayout_kernel, tt, batch),
        grid=(1, t_len // tt),
        in_specs=[pl.BlockSpec(memory_space=pl.ANY)] + weight_specs,
        out_specs=[pl.BlockSpec(memory_space=pl.ANY), _tm_state_spec(POOL_BUF, batch, d_a),
                   _tm_state_spec(conv_k - 1, batch, d_b)],
        out_shape=[jax.ShapeDtypeStruct(x.shape, F32),
                   jax.ShapeDtypeStruct((POOL_BUF, batch, d_a), F32),
                   jax.ShapeDtypeStruct((conv_k - 1, batch, d_b), F32)],
        scratch_shapes=_relayout_scratch(tt, batch, d_model) + _ab_scratch(tt, batch, d_a, d_b, conv_k),
        compiler_params=_PARAMS,
        name=name,
    )(x, *weights)


def _ffn_scratch(tt, bb, d_model, d_ff, n_keep):
    rows = tt * bb
    return [pltpu.VMEM((rows, d_model), BF16),
            pltpu.VMEM((rows, d_model), F32),
            pltpu.VMEM((rows + n_keep * bb, FF_CHUNK), F32),
            pltpu.VMEM((rows + n_keep * bb, FF_CHUNK), F32),
            pltpu.VMEM((n_keep * bb, 2 * d_ff), F32)]


def _ffn_weights(w):
    return (w["ffn_norm"], w["w_up"], w["ffn_conv_w"], w["ffn_conv_b"], w["w_down"], w["final_norm"])


def _ffn_stateful(x, state, w, bb, final_norm, name):
    tt, batch, d_model = x.shape
    d_ff = _layer_shape(w["w_down"])[0]
    n_keep = _layer_shape(w["ffn_conv_w"])[0] - 1
    st_spec = _tm_state_spec(n_keep, bb, 2 * d_ff)
    weights, weight_specs = _resident(_ffn_weights(w))
    return pl.pallas_call(
        functools.partial(_ffn_kernel, tt, bb, True, final_norm),
        grid=(batch // bb, 1),
        in_specs=[_tm_spec(tt, bb, d_model), st_spec] + weight_specs,
        out_specs=[_tm_spec(tt, bb, d_model), st_spec],
        out_shape=[jax.ShapeDtypeStruct(x.shape, F32),
                   jax.ShapeDtypeStruct((n_keep, batch, 2 * d_ff), F32)],
        scratch_shapes=_ffn_scratch(tt, bb, d_model, d_ff, n_keep),
        compiler_params=_PARAMS,
        name=name,
    )(x, state, *weights)


def _ffn_relayout(x, w, tt, final_norm, name):
    batch, t_len, d_model = x.shape
    d_ff = _layer_shape(w["w_down"])[0]
    n_keep = _layer_shape(w["ffn_conv_w"])[0] - 1
    assert batch == SUBLANES and t_len % tt == 0
    weights, weight_specs = _resident(_ffn_weights(w))
    return pl.pallas_call(
        functools.partial(_ffn_relayout_kernel, tt, batch, final_norm),
        grid=(1, t_len // tt),
        in_specs=[pl.BlockSpec(memory_space=pl.ANY)] + weight_specs,
        out_specs=[pl.BlockSpec(memory_space=pl.ANY), _tm_state_spec(n_keep, batch, 2 * d_ff)],
        out_shape=[jax.ShapeDtypeStruct(x.shape, F32),
                   jax.ShapeDtypeStruct((n_keep, batch, 2 * d_ff), F32)],
        scratch_shapes=(_relayout_scratch(tt, batch, d_model)
                        + _ffn_scratch(tt, batch, d_model, d_ff, n_keep)),
        compiler_params=_PARAMS,
        name=name,
    )(x, *weights)


def _mixer_c(x, w, tt, bb, name):
    d_c = _layer_shape(w["w_out_c"])[0]
    if bb == 1:
        assert tt % CHUNK == 0
        batch, t_len, d_model = x.shape
        x_spec = _bm_spec(1, tt, d_model)
        v_spec = pl.BlockSpec((1, CHUNK, d_c), lambda bi, ti: (bi, 0, 0))
        v_shape = (batch, CHUNK, d_c)
        sgu = (w["sgu_w"], w["sgu_b_chunk"])
    else:
        t_len, batch, d_model = x.shape
        assert t_len == tt
        x_spec = _tm_spec(tt, bb, d_model)
        v_spec = _tm_spec(tt, bb, d_c)
        v_shape = (t_len, batch, d_c)
        sgu = (w["sgu_w_short"], w["sgu_b_short"])
    rows = tt * bb
    weights, weight_specs = _resident(
        (w["mix_norm"], w["w_in_c"], w["sgu_norm_g"], w["sgu_norm_b"]) + sgu + (w["w_out_c"],))
    return pl.pallas_call(
        functools.partial(_mixer_c_kernel, tt, bb),
        grid=(batch // bb, t_len // tt),
        in_specs=[x_spec] + weight_specs,
        out_specs=[x_spec, v_spec],
        out_shape=[jax.ShapeDtypeStruct(x.shape, F32), jax.ShapeDtypeStruct(v_shape, F32)],
        scratch_shapes=[pltpu.VMEM((rows, d_c), F32),
                        pltpu.VMEM((rows, d_c), BF16),
                        pltpu.VMEM((rows, d_c), F32)],
        compiler_params=_PARAMS,
        name=name,
    )(x, *weights)


def _tm(a):
    return jnp.swapaxes(a, 0, 1)


def kernel(x_prompt, x_sample, state_pool, state_conv, state_ffn, mix_norm, w_in_ab, pool_w, pool_scale, conv_w, conv_b, conv_norm_g, conv_norm_b, w_out_ab, w_in_c, sgu_norm_g, sgu_norm_b, sgu_w, sgu_bias, w_out_c, ffn_norm, w_up, ffn_conv_w, ffn_conv_b, w_down, final_norm):
    dec_seq = x_sample.shape[1]
    n_heads = sgu_w.shape[1]
    d_c = w_out_c.shape[1]
    dh = d_c // n_heads
    row = lambda a: a.reshape(1, -1)
    rows_of = lambda a: a.reshape(a.shape[0], 1, -1)
    fin = row(final_norm)
    mix_n, ffn_n, ffn_cb = rows_of(mix_norm), rows_of(ffn_norm), rows_of(ffn_conv_b)
    w_up_b, w_down_b = w_up.astype(BF16), w_down.astype(BF16)
    ffn = lambda layer: dict(ffn_norm=(ffn_n, layer), w_up=(w_up_b, layer), ffn_conv_w=(ffn_conv_w, layer),
                             ffn_conv_b=(ffn_cb, layer), w_down=(w_down_b, layer), final_norm=fin)
    l0 = dict(mix_norm=(mix_n, 0), w_in_ab=w_in_ab[0].astype(BF16), pool_w=pool_w[0].astype(BF16),
              pool_scale=row(pool_scale[0]), conv_w=conv_w[0], conv_b=row(conv_b[0]),
              conv_norm_g=row(conv_norm_g[0]), conv_norm_b=row(conv_norm_b[0]),
              w_out_ab=w_out_ab[0].astype(BF16), **ffn(0))
    sw_short = jnp.repeat(jnp.transpose(sgu_w[0][:, :dec_seq, :dec_seq], (1, 2, 0)), dh, axis=2)
    l1 = dict(mix_norm=(mix_n, 1), w_in_c=w_in_c[0].astype(BF16),
              sgu_norm_g=row(sgu_norm_g[0]), sgu_norm_b=row(sgu_norm_b[0]),
              sgu_w=sgu_w[0], sgu_b_chunk=jnp.repeat(sgu_bias[0].T, dh, axis=1),
              sgu_w_short=sw_short.reshape(dec_seq * dec_seq, d_c),
              sgu_b_short=jnp.repeat(sgu_bias[0][:, :dec_seq].T, dh, axis=1),
              w_out_c=w_out_c[0].astype(BF16), **ffn(1))

    xp, pool_p, conv_p = _mixer_ab_relayout(x_prompt, l0, 64, "mixer_ab_prompt")
    xp, f0_p = _ffn_relayout(xp, l0, 64, False, "ffn0_prompt")
    xp, v_p = _mixer_c(xp, l1, 512, 1, "mixer_c_prompt")
    yp, f1_p = _ffn_relayout(xp, l1, 64, True, "ffn1_prompt")

    bb_s = 64
    xs, pool_s, conv_s = _mixer_ab_stateful(_tm(x_sample), (_tm(state_pool[0]), _tm(state_conv[0])),
                                            l0, bb_s, PAST_LEN, "mixer_ab_sample")
    xs, f0_s = _ffn_stateful(xs, _tm(state_ffn[0]), l0, bb_s, False, "ffn0_sample")
    xs, v_s = _mixer_c(xs, l1, dec_seq, bb_s, "mixer_c_sample")
    ys, f1_s = _ffn_stateful(xs, _tm(state_ffn[1]), l1, bb_s, True, "ffn1_sample")

    return (yp, _tm(ys), _tm(pool_p)[None], _tm(pool_s)[None], _tm(conv_p)[None], _tm(conv_s)[None],
            v_p[None], _tm(v_s)[None],
            jnp.stack([_tm(f0_p), _tm(f1_p)]), jnp.stack([_tm(f0_s), _tm(f1_s)]))
```

```python
import functools

import jax
import jax.numpy as jnp
from jax import lax
from jax.experimental import pallas as pl
from jax.experimental.pallas import tpu as pltpu

F32 = jnp.float32
BF16 = jnp.bfloat16

EPS = 1e-6
POOL_WINDOWS = (2, 4, 8, 16)
POOL_BUF = max(POOL_WINDOWS) - 1
CHUNK = 128
PAST_LEN = 16384
LANES = 128
SUBLANES = 8
FF_CHUNK = 256
DOWN_GROUP = 4
UP_AHEAD = 3
CONV_ROWS = 64
VMEM_LIMIT = 56 * 1024 * 1024


def _rmsnorm(x, g):
    ms = jnp.mean(x * x, axis=-1, keepdims=True)
    return x * lax.rsqrt(ms + EPS) * g


def _layernorm(x, g, b):
    mu = jnp.mean(x, axis=-1, keepdims=True)
    xc = x - mu
    var = jnp.mean(xc * xc, axis=-1, keepdims=True)
    return xc * lax.rsqrt(var + EPS) * g + b


def _silu(x):
    return x * jax.nn.sigmoid(x)


def _bdot(a, b):
    return jnp.dot(a, b, preferred_element_type=F32)


def _ab_front(h, win_ref, pw_ref, ps_ref, xa_full, bb, pos0):
    rows = h.shape[0]
    d_a = xa_full.shape[1]
    hp = POOL_BUF * bb
    proj = _bdot(h, win_ref[...])
    d_b = (proj.shape[1] - d_a) // 2
    xa = proj[:, :d_a]
    glu = proj[:, d_a:d_a + d_b] * jax.nn.sigmoid(proj[:, d_a + d_b:])
    xa_full[hp:hp + rows, :] = xa
    pos = pos0 + lax.broadcasted_iota(jnp.int32, (rows, LANES), 0) // bb
    group = d_a // len(POOL_WINDOWS)
    ya = []
    for gi, w in enumerate(POOL_WINDOWS):
        c0 = gi * group
        xg = xa[:, c0:c0 + group]
        win = xg
        for k in range(1, w):
            win = win + xa_full[hp - k * bb:hp - k * bb + rows, c0:c0 + group]
        cnt = jnp.minimum(pos + 1, w).astype(F32)
        d = (win / cnt - xg).astype(BF16)
        ya.append((_bdot(d, pw_ref[gi]) * ps_ref[:, c0:c0 + group]).astype(BF16))
    return glu, jnp.concatenate(ya, axis=1)


def _conv_tile_dense(glu_full, c_ref, r0, lanes, taps, bias):
    n_out = CONV_ROWS // SUBLANES
    acc = [bias] * n_out
    for m in range(n_out + len(taps) - 1):
        grp = glu_full[pl.ds(r0 + m * SUBLANES, SUBLANES), lanes]
        for k in range(len(taps)):
            if 0 <= m - k < n_out:
                acc[m - k] = acc[m - k] + grp * taps[k]
    for i in range(n_out):
        c_ref[pl.ds(r0 + i * SUBLANES, SUBLANES), lanes] = acc[i]


def _depthwise_conv(glu_full, c_ref, rows, bb, cw_ref, cb_ref):
    n_taps = cw_ref.shape[0]
    n_blocks = rows // CONV_ROWS
    if bb != SUBLANES:
        def block(r, carry):
            r0 = pl.multiple_of(r * CONV_ROWS, CONV_ROWS)
            acc = jnp.broadcast_to(cb_ref[...], (CONV_ROWS, c_ref.shape[1]))
            for k in range(n_taps):
                acc = acc + glu_full[pl.ds(r0 + k * bb, CONV_ROWS), :] * cw_ref[k:k + 1, :]
            c_ref[pl.ds(r0, CONV_ROWS), :] = acc
            return carry

        lax.fori_loop(0, n_blocks, block, 0)
        return
    for lane0 in range(0, c_ref.shape[1], LANES):
        lanes = slice(lane0, lane0 + LANES)
        taps = [jnp.broadcast_to(cw_ref[k:k + 1, lanes], (SUBLANES, LANES)) for k in range(n_taps)]
        bias = jnp.broadcast_to(cb_ref[:, lanes], (SUBLANES, LANES))

        def block(r, carry, lanes=lanes, taps=taps, bias=bias):
            r0 = pl.multiple_of(r * CONV_ROWS, CONV_ROWS)
            _conv_tile_dense(glu_full, c_ref, r0, lanes, taps, bias)
            return carry

        lax.fori_loop(0, n_blocks, block, 0)


def _mixer_ab_kernel(tt, bb, has_state, start_pos, *refs):
    x_ref, refs = refs[0], refs[1:]
    if has_state:
        (sp_ref, sc_ref), refs = refs[:2], refs[2:]
    (g_ref, win_ref, pw_ref, ps_ref, cw_ref, cb_ref, lg_ref, lb_ref, wout_ref,
     xo_ref, po_ref, co_ref, xa_full, glu_full, c_ref) = refs
    rows = tt * bb
    hp = POOL_BUF * bb
    hc = (cw_ref.shape[0] - 1) * bb
    ti = pl.program_id(1)

    if has_state:
        xa_full[0:hp, :] = sp_ref[...].reshape(hp, xa_full.shape[1])
        glu_full[0:hc, :] = sc_ref[...].reshape(hc, glu_full.shape[1])
    else:
        @pl.when(ti == 0)
        def _():
            xa_full[...] = jnp.zeros(xa_full.shape, F32)
            glu_full[...] = jnp.zeros(glu_full.shape, F32)

        xa_full[0:hp, :] = xa_full[rows:rows + hp, :]
        glu_full[0:hc, :] = glu_full[rows:rows + hc, :]

    x = x_ref[...].reshape(rows, x_ref.shape[2])
    h = _rmsnorm(x, g_ref[...]).astype(BF16)
    glu, ya = _ab_front(h, win_ref, pw_ref, ps_ref, xa_full, bb, start_pos + ti * tt)
    glu_full[hc:hc + rows, :] = glu
    po_ref[...] = xa_full[rows:rows + hp, :].reshape(po_ref.shape)
    co_ref[...] = glu_full[rows:rows + hc, :].reshape(co_ref.shape)

    _depthwise_conv(glu_full, c_ref, rows, bb, cw_ref, cb_ref)
    yb = _silu(_layernorm(c_ref[...], lg_ref[...], lb_ref[...])).astype(BF16)
    out = x + _bdot(jnp.concatenate([ya, yb], axis=1), wout_ref[...])
    xo_ref[...] = out.reshape(xo_ref.shape)


def _mixer_ab_stateful_kernel(tt, bb, start_pos, x_hbm, *refs):
    rest, (x_buf, x_sem), scratch = refs[:14], refs[14:16], refs[16:]
    loads = _step_copies(x_hbm, x_buf, x_sem, pl.program_id(0) * bb, False)
    for c in loads:
        c.start()
    for c in loads:
        c.wait()
    _mixer_ab_kernel(tt, bb, True, start_pos, x_buf, *rest, *scratch)


def _ffn_conv_slabs(full_ref, up, halo_rows, fw, fb, bb, rows):
    halo = halo_rows.shape[0]
    full_ref[0:halo, :] = halo_rows
    full_ref[halo:halo + rows, :] = up
    tail = full_ref[rows:rows + halo, :]
    cv = (full_ref[0:rows, :] * fw[0:1, :] + full_ref[bb:bb + rows, :] * fw[1:2, :]
          + up * fw[2:3, :] + fb)
    return cv, tail


def _ffn_kernel(tt, bb, has_state, final_norm, *refs):
    x_ref, refs = refs[0], refs[1:]
    if has_state:
        st_ref, refs = refs[0], refs[1:]
    (g_ref, wup_ref, fw_ref, fb_ref, wdown_ref, fin_ref,
     xo_ref, so_ref, h_ref, acc_ref, full_a_ref, full_g_ref, carry_ref) = refs
    rows = tt * bb
    d_ff = wdown_ref.shape[0]
    n_keep = fw_ref.shape[0] - 1
    ti = pl.program_id(1)

    d_model = x_ref.shape[2]
    h_ref[...] = _rmsnorm(x_ref[...].reshape(rows, d_model), g_ref[...]).astype(BF16)

    if not has_state:
        @pl.when(ti == 0)
        def _():
            carry_ref[...] = jnp.zeros(carry_ref.shape, F32)

    def up_pair(c):
        return [_bdot(h_ref[...], wup_ref[:, off:off + FF_CHUNK])
                for off in (c * FF_CHUNK, d_ff + c * FF_CHUNK)]

    n_chunks = d_ff // FF_CHUNK
    ups_ahead = [up_pair(c) for c in range(UP_AHEAD)]
    acts = []
    for c in range(n_chunks):
        ups = ups_ahead.pop(0)
        if c + UP_AHEAD < n_chunks:
            ups_ahead.append(up_pair(c + UP_AHEAD))
        col_pair = [slice(off, off + FF_CHUNK) for off in (c * FF_CHUNK, d_ff + c * FF_CHUNK)]
        conv = []
        for full_ref, cols, up in zip((full_a_ref, full_g_ref), col_pair, ups):
            if has_state:
                halo_rows = st_ref[:, :, cols].reshape(n_keep * bb, FF_CHUNK)
            else:
                halo_rows = carry_ref[:, cols]
            cv, tail = _ffn_conv_slabs(full_ref, up, halo_rows, fw_ref[:, cols], fb_ref[:, cols], bb, rows)
            so_ref[:, :, cols] = tail.reshape(n_keep, bb, FF_CHUNK)
            if not has_state:
                carry_ref[:, cols] = tail
            conv.append(cv)
        acts.append((_silu(conv[0]) * conv[1]).astype(BF16))
        if len(acts) == DOWN_GROUP or c + 1 == n_chunks:
            lo = (c + 1 - len(acts)) * FF_CHUNK
            act = acts[0] if len(acts) == 1 else jnp.concatenate(acts, axis=1)
            base = x_ref[...].reshape(rows, d_model) if lo == 0 else acc_ref[...]
            total = base + _bdot(act, wdown_ref[lo:(c + 1) * FF_CHUNK, :])
            if c + 1 < n_chunks:
                acc_ref[...] = total
            else:
                if final_norm:
                    total = _rmsnorm(total, fin_ref[...])
                xo_ref[...] = total.reshape(xo_ref.shape)
            acts = []


def _tile_copies(hbm, buf, sems, slot, step, tt, to_hbm):
    copies = []
    for b in range(buf.shape[2]):
        rows_hbm = hbm.at[b, pl.ds(step * tt, tt), :]
        rows_vmem = buf.at[slot, :, b, :]
        src, dst = (rows_vmem, rows_hbm) if to_hbm else (rows_hbm, rows_vmem)
        copies.append(pltpu.make_async_copy(src, dst, sems.at[slot, b]))
    return copies


def _with_relayout(tt, x_hbm, xo_hbm, xbuf, obuf, in_sem, out_sem, compute):
    s = pl.program_id(1)
    n_steps = pl.num_programs(1)
    slot = s % 2

    def start(copies):
        for c in copies:
            c.start()

    def wait(copies):
        for c in copies:
            c.wait()

    @pl.when(s == 0)
    def _():
        start(_tile_copies(x_hbm, xbuf, in_sem, 0, 0, tt, False))

    @pl.when(s + 1 < n_steps)
    def _():
        start(_tile_copies(x_hbm, xbuf, in_sem, 1 - slot, s + 1, tt, False))

    wait(_tile_copies(x_hbm, xbuf, in_sem, slot, s, tt, False))

    @pl.when(s >= 2)
    def _():
        wait(_tile_copies(xo_hbm, obuf, out_sem, slot, s - 2, tt, True))

    compute(xbuf.at[slot], obuf.at[slot])
    start(_tile_copies(xo_hbm, obuf, out_sem, slot, s, tt, True))

    @pl.when(s == n_steps - 1)
    def _():
        @pl.when(s >= 1)
        def _():
            wait(_tile_copies(xo_hbm, obuf, out_sem, 1 - slot, s - 1, tt, True))
        wait(_tile_copies(xo_hbm, obuf, out_sem, slot, s, tt, True))


def _ffn_relayout_kernel(tt, bb, final_norm, x_hbm, *refs):
    weights, (xo_hbm, so_ref, xbuf, obuf, in_sem, out_sem), scratch = refs[:6], refs[6:12], refs[12:]
    _with_relayout(tt, x_hbm, xo_hbm, xbuf, obuf, in_sem, out_sem,
                   lambda x_tile, out_tile: _ffn_kernel(tt, bb, False, final_norm, x_tile, *weights,
                                                        out_tile, so_ref, *scratch))


def _mixer_ab_relayout_kernel(tt, bb, x_hbm, *refs):
    weights, (xo_hbm, po_ref, co_ref, xbuf, obuf, in_sem, out_sem), scratch = refs[:9], refs[9:16], refs[16:]
    _with_relayout(tt, x_hbm, xo_hbm, xbuf, obuf, in_sem, out_sem,
                   lambda x_tile, out_tile: _mixer_ab_kernel(tt, bb, False, 0, x_tile, *weights,
                                                             out_tile, po_ref, co_ref, *scratch))


def _step_copies(hbm, buf, sems, b0, to_hbm):
    copies = []
    for t in range(buf.shape[0]):
        rows_hbm = hbm.at[pl.ds(b0, buf.shape[1]), t, :]
        src, dst = (buf.at[t], rows_hbm) if to_hbm else (rows_hbm, buf.at[t])
        copies.append(pltpu.make_async_copy(src, dst, sems.at[t]))
    return copies


def _ffn_stateful_kernel(tt, bb, layer, final_norm, *refs):
    x_ref, st_hbm, refs = refs[0], refs[1], refs[2:]
    if layer:
        prev_hbm, refs = refs[0], refs[1:]
    weights, (xo_ref, so_hbm, st_buf, so_buf, st_sem, so_sem), refs = refs[:6], refs[6:12], refs[12:]
    if layer:
        prev_sem, refs = refs[0], refs[1:]
    if final_norm:
        (y_buf, y_sem), refs = refs[:2], refs[2:]
    step = pl.program_id(0)
    b0 = step * bb

    if layer:
        carry_over = pltpu.make_async_copy(prev_hbm, so_hbm.at[pl.ds(0, layer)], prev_sem.at[0])

        @pl.when(step == 0)
        def _():
            carry_over.start()

    loads = _step_copies(st_hbm.at[layer], st_buf, st_sem, b0, False)
    for c in loads:
        c.start()
    for c in loads:
        c.wait()
    _ffn_kernel(tt, bb, True, final_norm, x_ref, st_buf, *weights,
                y_buf if final_norm else xo_ref, so_buf, *refs)
    stores = _step_copies(so_hbm.at[layer], so_buf, so_sem, b0, True)
    if final_norm:
        stores += _step_copies(xo_ref, y_buf, y_sem, b0, True)
    for c in stores:
        c.start()
    for c in stores:
        c.wait()

    if layer:
        @pl.when(step == pl.num_programs(0) - 1)
        def _():
            carry_over.wait()


def _mixer_c_kernel(tt, bb, *refs):
    (x_ref, g_ref, win_ref, lg_ref, lb_ref, sw_ref, sb_ref, wout_ref,
     xo_ref, vo_ref, v_ref, vb_ref, z_ref) = refs
    rows = tt * bb
    d_c = wout_ref.shape[0]

    x = x_ref[...].reshape(rows, x_ref.shape[2])
    h = _rmsnorm(x, g_ref[...]).astype(BF16)
    v = _layernorm(_bdot(h, win_ref[:, d_c:]), lg_ref[...], lb_ref[...])
    u = _bdot(h, win_ref[:, :d_c])

    if bb == 1:
        n_heads = sw_ref.shape[0]
        dh = d_c // n_heads
        n_chunks = tt // CHUNK
        vo_ref[0] = v[rows - CHUNK:, :]
        vb_ref[...] = v.astype(BF16)
        row_i = lax.broadcasted_iota(jnp.int32, (CHUNK, CHUNK), 0)
        col_i = lax.broadcasted_iota(jnp.int32, (CHUNK, CHUNK), 1)
        for hd in range(n_heads):
            hcols = slice(hd * dh, (hd + 1) * dh)
            m = jnp.where(col_i <= row_i, sw_ref[hd], 0.0).astype(BF16)
            rhs = jnp.concatenate(
                [vb_ref[c * CHUNK:(c + 1) * CHUNK, hcols] for c in range(n_chunks)], axis=1)
            zc = _bdot(m, rhs)
            bias = sb_ref[:, hcols]
            for c in range(n_chunks):
                z_ref[c * CHUNK:(c + 1) * CHUNK, hcols] = zc[:, c * dh:(c + 1) * dh] + bias
    else:
        vo_ref[...] = v.reshape(vo_ref.shape)
        v_ref[...] = v
        for i in range(tt):
            zi = jnp.broadcast_to(sb_ref[i:i + 1, :], (bb, d_c))
            for j in range(i + 1):
                zi = zi + v_ref[j * bb:(j + 1) * bb, :] * sw_ref[i * tt + j:i * tt + j + 1, :]
            z_ref[i * bb:(i + 1) * bb, :] = zi

    gated = (u * z_ref[...]).astype(BF16)
    out = x + _bdot(gated, wout_ref[...])
    xo_ref[...] = out.reshape(xo_ref.shape)


def _mixer_c_short_kernel(tt, bb, *refs):
    ins, (xo_ref, v_hbm, v_buf, v_sem), scratch = refs[:8], refs[8:12], refs[12:]
    _mixer_c_kernel(tt, bb, *ins, xo_ref, v_buf, *scratch)
    stores = _step_copies(v_hbm, v_buf, v_sem, pl.program_id(0) * bb, True)
    for c in stores:
        c.start()
    for c in stores:
        c.wait()


def _layer_shape(entry):
    return entry[0].shape[1:] if isinstance(entry, tuple) else entry.shape


def _resident(entries):
    arrays, specs = [], []
    for entry in entries:
        arr, layer = entry if isinstance(entry, tuple) else (entry, None)
        if layer is None:
            block, index = arr.shape, (0,) * arr.ndim
        else:
            block, index = (None,) + arr.shape[1:], (layer,) + (0,) * (arr.ndim - 1)
        arrays.append(arr)
        specs.append(pl.BlockSpec(block, lambda bi, ti, index=index: index,
                                  pipeline_mode=pl.Buffered(1)))
    return arrays, specs


def _tm_spec(tt, bb, width):
    return pl.BlockSpec((tt, bb, width), lambda bi, ti: (ti, bi, 0))


def _bm_spec(bb, tt, width):
    return pl.BlockSpec((bb, tt, width), lambda bi, ti: (bi, ti, 0))


def _tm_state_spec(n, bb, width):
    return pl.BlockSpec((n, bb, width), lambda bi, ti: (0, bi, 0))


_PARAMS = pltpu.CompilerParams(
    dimension_semantics=("arbitrary", "arbitrary"), vmem_limit_bytes=VMEM_LIMIT)


def _ab_weights(w):
    return (w["mix_norm"], w["w_in_ab"], w["pool_w"], w["pool_scale"], w["conv_w"], w["conv_b"],
            w["conv_norm_g"], w["conv_norm_b"], w["w_out_ab"])


def _ab_dims(w):
    return (_layer_shape(w["pool_scale"])[1], _layer_shape(w["conv_b"])[1], _layer_shape(w["conv_w"])[0])


def _ab_scratch(tt, bb, d_a, d_b, conv_k):
    return [pltpu.VMEM(((tt + POOL_BUF) * bb, d_a), F32),
            pltpu.VMEM(((tt + conv_k - 1) * bb, d_b), F32),
            pltpu.VMEM((tt * bb, d_b), F32)]


def _relayout_scratch(tt, batch, d_model):
    return [pltpu.VMEM((2, tt, batch, d_model), F32), pltpu.VMEM((2, tt, batch, d_model), F32),
            pltpu.SemaphoreType.DMA((2, batch)), pltpu.SemaphoreType.DMA((2, batch))]


def _mixer_ab_stateful(x, states, w, bb, start_pos, name):
    batch, tt, d_model = x.shape
    d_a, d_b, conv_k = _ab_dims(w)
    state_specs = [_tm_state_spec(POOL_BUF, bb, d_a), _tm_state_spec(conv_k - 1, bb, d_b)]
    weights, weight_specs = _resident(_ab_weights(w))
    return pl.pallas_call(
        functools.partial(_mixer_ab_stateful_kernel, tt, bb, start_pos),
        grid=(batch // bb, 1),
        in_specs=[pl.BlockSpec(memory_space=pl.ANY)] + state_specs + weight_specs,
        out_specs=[_tm_spec(tt, bb, d_model)] + state_specs,
        out_shape=[jax.ShapeDtypeStruct((tt, batch, d_model), F32),
                   jax.ShapeDtypeStruct((POOL_BUF, batch, d_a), F32),
                   jax.ShapeDtypeStruct((conv_k - 1, batch, d_b), F32)],
        scratch_shapes=([pltpu.VMEM((tt, bb, d_model), F32), pltpu.SemaphoreType.DMA((tt,))]
                        + _ab_scratch(tt, bb, d_a, d_b, conv_k)),
        compiler_params=_PARAMS,
        name=name,
    )(x, *states, *weights)


def _mixer_ab_relayout(x, w, tt, name):
    batch, t_len, d_model = x.shape
    d_a, d_b, conv_k = _ab_dims(w)
    assert batch == SUBLANES and t_len % tt == 0 and tt >= conv_k - 1
    weights, weight_specs = _resident(_ab_weights(w))
    return pl.pallas_call(
        functools.partial(_mixer_ab_relayout_kernel, tt, batch),
        grid=(1, t_len // tt),
        in_specs=[pl.BlockSpec(memory_space=pl.ANY)] + weight_specs,
        out_specs=[pl.BlockSpec(memory_space=pl.ANY), _tm_state_spec(POOL_BUF, batch, d_a),
                   _tm_state_spec(conv_k - 1, batch, d_b)],
        out_shape=[jax.ShapeDtypeStruct(x.shape, F32),
                   jax.ShapeDtypeStruct((POOL_BUF, batch, d_a), F32),
                   jax.ShapeDtypeStruct((conv_k - 1, batch, d_b), F32)],
        scratch_shapes=_relayout_scratch(tt, batch, d_model) + _ab_scratch(tt, batch, d_a, d_b, conv_k),
        compiler_params=_PARAMS,
        name=name,
    )(x, *weights)


def _ffn_scratch(tt, bb, d_model, d_ff, n_keep):
    rows = tt * bb
    return [pltpu.VMEM((rows, d_model), BF16),
            pltpu.VMEM((rows, d_model), F32),
            pltpu.VMEM((rows + n_keep * bb, FF_CHUNK), F32),
            pltpu.VMEM((rows + n_keep * bb, FF_CHUNK), F32),
            pltpu.VMEM((n_keep * bb, 2 * d_ff), F32)]


def _ffn_weights(w):
    return (w["ffn_norm"], w["w_up"], w["ffn_conv_w"], w["ffn_conv_b"], w["w_down"], w["final_norm"])


def _ffn_stateful(x, states, prev_out, w, layer, bb, final_norm, name):
    tt, batch, d_model = x.shape
    d_ff = _layer_shape(w["w_down"])[0]
    n_keep = _layer_shape(w["ffn_conv_w"])[0] - 1
    assert (prev_out is None) == (layer == 0)
    hbm = pl.BlockSpec(memory_space=pl.ANY)
    weights, weight_specs = _resident(_ffn_weights(w))
    state_bufs = [pltpu.VMEM((n_keep, bb, 2 * d_ff), F32), pltpu.VMEM((n_keep, bb, 2 * d_ff), F32),
                  pltpu.SemaphoreType.DMA((n_keep,)), pltpu.SemaphoreType.DMA((n_keep,))]
    prev_bufs = [pltpu.SemaphoreType.DMA((1,))] if layer else []
    y_bufs = [pltpu.VMEM((tt, bb, d_model), F32), pltpu.SemaphoreType.DMA((tt,))] if final_norm else []
    x_out_spec = hbm if final_norm else _tm_spec(tt, bb, d_model)
    x_out_shape = (batch, tt, d_model) if final_norm else x.shape
    return pl.pallas_call(
        functools.partial(_ffn_stateful_kernel, tt, bb, layer, final_norm),
        grid=(batch // bb, 1),
        in_specs=[_tm_spec(tt, bb, d_model), hbm] + ([hbm] if layer else []) + weight_specs,
        out_specs=[x_out_spec, hbm],
        out_shape=[jax.ShapeDtypeStruct(x_out_shape, F32),
                   jax.ShapeDtypeStruct((layer + 1, batch, n_keep, 2 * d_ff), F32)],
        scratch_shapes=state_bufs + prev_bufs + y_bufs + _ffn_scratch(tt, bb, d_model, d_ff, n_keep),
        compiler_params=_PARAMS,
        name=name,
    )(x, states, *(() if prev_out is None else (prev_out,)), *weights)


def _ffn_relayout(x, w, tt, final_norm, name):
    batch, t_len, d_model = x.shape
    d_ff = _layer_shape(w["w_down"])[0]
    n_keep = _layer_shape(w["ffn_conv_w"])[0] - 1
    assert batch == SUBLANES and t_len % tt == 0
    weights, weight_specs = _resident(_ffn_weights(w))
    return pl.pallas_call(
        functools.partial(_ffn_relayout_kernel, tt, batch, final_norm),
        grid=(1, t_len // tt),
        in_specs=[pl.BlockSpec(memory_space=pl.ANY)] + weight_specs,
        out_specs=[pl.BlockSpec(memory_space=pl.ANY), _tm_state_spec(n_keep, batch, 2 * d_ff)],
        out_shape=[jax.ShapeDtypeStruct(x.shape, F32),
                   jax.ShapeDtypeStruct((n_keep, batch, 2 * d_ff), F32)],
        scratch_shapes=(_relayout_scratch(tt, batch, d_model)
                        + _ffn_scratch(tt, batch, d_model, d_ff, n_keep)),
        compiler_params=_PARAMS,
        name=name,
    )(x, *weights)


def _mixer_c(x, w, tt, bb, name):
    d_c = _layer_shape(w["w_out_c"])[0]
    extra_scratch = []
    if bb == 1:
        assert tt % CHUNK == 0
        batch, t_len, d_model = x.shape
        body = _mixer_c_kernel
        x_spec = _bm_spec(1, tt, d_model)
        v_spec = pl.BlockSpec((1, CHUNK, d_c), lambda bi, ti: (bi, 0, 0))
        v_shape = (batch, CHUNK, d_c)
        sgu = (w["sgu_w"], w["sgu_b_chunk"])
    else:
        t_len, batch, d_model = x.shape
        assert t_len == tt
        body = _mixer_c_short_kernel
        x_spec = _tm_spec(tt, bb, d_model)
        v_spec = pl.BlockSpec(memory_space=pl.ANY)
        v_shape = (batch, t_len, d_c)
        sgu = (w["sgu_w_short"], w["sgu_b_short"])
        extra_scratch = [pltpu.VMEM((tt, bb, d_c), F32), pltpu.SemaphoreType.DMA((tt,))]
    rows = tt * bb
    weights, weight_specs = _resident(
        (w["mix_norm"], w["w_in_c"], w["sgu_norm_g"], w["sgu_norm_b"]) + sgu + (w["w_out_c"],))
    return pl.pallas_call(
        functools.partial(body, tt, bb),
        grid=(batch // bb, t_len // tt),
        in_specs=[x_spec] + weight_specs,
        out_specs=[x_spec, v_spec],
        out_shape=[jax.ShapeDtypeStruct(x.shape, F32), jax.ShapeDtypeStruct(v_shape, F32)],
        scratch_shapes=extra_scratch + [pltpu.VMEM((rows, d_c), F32),
                                        pltpu.VMEM((rows, d_c), BF16),
                                        pltpu.VMEM((rows, d_c), F32)],
        compiler_params=_PARAMS,
        name=name,
    )(x, *weights)


def _tm(a):
    return jnp.swapaxes(a, 0, 1)


def kernel(x_prompt, x_sample, state_pool, state_conv, state_ffn, mix_norm, w_in_ab, pool_w, pool_scale, conv_w, conv_b, conv_norm_g, conv_norm_b, w_out_ab, w_in_c, sgu_norm_g, sgu_norm_b, sgu_w, sgu_bias, w_out_c, ffn_norm, w_up, ffn_conv_w, ffn_conv_b, w_down, final_norm):
    dec_seq = x_sample.shape[1]
    n_heads = sgu_w.shape[1]
    d_c = w_out_c.shape[1]
    dh = d_c // n_heads
    row = lambda a: a.reshape(1, -1)
    rows_of = lambda a: a.reshape(a.shape[0], 1, -1)
    fin = row(final_norm)
    mix_n, ffn_n, ffn_cb = rows_of(mix_norm), rows_of(ffn_norm), rows_of(ffn_conv_b)
    w_up_b, w_down_b = w_up.astype(BF16), w_down.astype(BF16)
    ffn = lambda layer: dict(ffn_norm=(ffn_n, layer), w_up=(w_up_b, layer), ffn_conv_w=(ffn_conv_w, layer),
                             ffn_conv_b=(ffn_cb, layer), w_down=(w_down_b, layer), final_norm=fin)
    l0 = dict(mix_norm=(mix_n, 0), w_in_ab=w_in_ab[0].astype(BF16), pool_w=pool_w[0].astype(BF16),
              pool_scale=row(pool_scale[0]), conv_w=conv_w[0], conv_b=row(conv_b[0]),
              conv_norm_g=row(conv_norm_g[0]), conv_norm_b=row(conv_norm_b[0]),
              w_out_ab=w_out_ab[0].astype(BF16), **ffn(0))
    sw_short = jnp.repeat(jnp.transpose(sgu_w[0][:, :dec_seq, :dec_seq], (1, 2, 0)), dh, axis=2)
    l1 = dict(mix_norm=(mix_n, 1), w_in_c=w_in_c[0].astype(BF16),
              sgu_norm_g=row(sgu_norm_g[0]), sgu_norm_b=row(sgu_norm_b[0]),
              sgu_w=sgu_w[0], sgu_b_chunk=jnp.repeat(sgu_bias[0].T, dh, axis=1),
              sgu_w_short=sw_short.reshape(dec_seq * dec_seq, d_c),
              sgu_b_short=jnp.repeat(sgu_bias[0][:, :dec_seq].T, dh, axis=1),
              w_out_c=w_out_c[0].astype(BF16), **ffn(1))

    xp, pool_p, conv_p = _mixer_ab_relayout(x_prompt, l0, 128, "mixer_ab_prompt")
    xp, f0_p = _ffn_relayout(xp, l0, 64, False, "ffn0_prompt")
    xp, v_p = _mixer_c(xp, l1, 512, 1, "mixer_c_prompt")
    yp, f1_p = _ffn_relayout(xp, l1, 64, True, "ffn1_prompt")

    bb_s = 64
    xs, pool_s, conv_s = _mixer_ab_stateful(x_sample, (_tm(state_pool[0]), _tm(state_conv[0])),
                                            l0, bb_s, PAST_LEN, "mixer_ab_sample")
    xs, ffn_s = _ffn_stateful(xs, state_ffn, None, l0, 0, bb_s, False, "ffn0_sample")
    xs, v_s = _mixer_c(xs, l1, dec_seq, bb_s, "mixer_c_sample")
    ys, ffn_s = _ffn_stateful(xs, state_ffn, ffn_s, l1, 1, bb_s, True, "ffn1_sample")

    return (yp, ys, _tm(pool_p)[None], _tm(pool_s)[None], _tm(conv_p)[None], _tm(conv_s)[None],
            v_p[None], v_s[None],
            jnp.stack([_tm(f0_p), _tm(f1_p)]), ffn_s)
```

```python
import functools

import jax
import jax.numpy as jnp
from jax import lax
from jax.experimental import pallas as pl
from jax.experimental.pallas import tpu as pltpu

F32 = jnp.float32
BF16 = jnp.bfloat16

EPS = 1e-6
POOL_WINDOWS = (2, 4, 8, 16)
POOL_BUF = max(POOL_WINDOWS) - 1
CHUNK = 128
PAST_LEN = 16384
LANES = 128
SUBLANES = 8
FF_CHUNK = 256
DOWN_GROUP = 4
UP_AHEAD = 3
CONV_ROWS = 64
VMEM_LIMIT = 56 * 1024 * 1024


def _rmsnorm(x, g):
    ms = jnp.mean(x * x, axis=-1, keepdims=True)
    return x * lax.rsqrt(ms + EPS) * g


def _layernorm(x, g, b):
    mu = jnp.mean(x, axis=-1, keepdims=True)
    xc = x - mu
    var = jnp.mean(xc * xc, axis=-1, keepdims=True)
    return xc * lax.rsqrt(var + EPS) * g + b


def _silu(x):
    return x * jax.nn.sigmoid(x)


def _bdot(a, b):
    return jnp.dot(a, b, preferred_element_type=F32)


def _ab_front(h, win_ref, pw_ref, ps_ref, xa_full, bb, pos0):
    rows = h.shape[0]
    d_a = xa_full.shape[1]
    hp = POOL_BUF * bb
    proj = _bdot(h, win_ref[...])
    d_b = (proj.shape[1] - d_a) // 2
    xa = proj[:, :d_a]
    glu = proj[:, d_a:d_a + d_b] * jax.nn.sigmoid(proj[:, d_a + d_b:])
    xa_full[hp:hp + rows, :] = xa
    pos = pos0 + lax.broadcasted_iota(jnp.int32, (rows, LANES), 0) // bb
    group = d_a // len(POOL_WINDOWS)
    ya = []
    for gi, w in enumerate(POOL_WINDOWS):
        c0 = gi * group
        xg = xa[:, c0:c0 + group]
        win = xg
        for k in range(1, w):
            win = win + xa_full[hp - k * bb:hp - k * bb + rows, c0:c0 + group]
        cnt = jnp.minimum(pos + 1, w).astype(F32)
        d = (win / cnt - xg).astype(BF16)
        ya.append((_bdot(d, pw_ref[gi]) * ps_ref[:, c0:c0 + group]).astype(BF16))
    return glu, jnp.concatenate(ya, axis=1)


def _conv_tile_dense(glu_full, c_ref, r0, lanes, taps, bias):
    n_out = CONV_ROWS // SUBLANES
    acc = [bias] * n_out
    for m in range(n_out + len(taps) - 1):
        grp = glu_full[pl.ds(r0 + m * SUBLANES, SUBLANES), lanes]
        for k in range(len(taps)):
            if 0 <= m - k < n_out:
                acc[m - k] = acc[m - k] + grp * taps[k]
    for i in range(n_out):
        c_ref[pl.ds(r0 + i * SUBLANES, SUBLANES), lanes] = acc[i]


def _depthwise_conv(glu_full, c_ref, rows, bb, cw_ref, cb_ref):
    n_taps = cw_ref.shape[0]
    n_blocks = rows // CONV_ROWS
    if bb != SUBLANES:
        def block(r, carry):
            r0 = pl.multiple_of(r * CONV_ROWS, CONV_ROWS)
            acc = jnp.broadcast_to(cb_ref[...], (CONV_ROWS, c_ref.shape[1]))
            for k in range(n_taps):
                acc = acc + glu_full[pl.ds(r0 + k * bb, CONV_ROWS), :] * cw_ref[k:k + 1, :]
            c_ref[pl.ds(r0, CONV_ROWS), :] = acc
            return carry

        lax.fori_loop(0, n_blocks, block, 0)
        return
    for lane0 in range(0, c_ref.shape[1], LANES):
        lanes = slice(lane0, lane0 + LANES)
        taps = [jnp.broadcast_to(cw_ref[k:k + 1, lanes], (SUBLANES, LANES)) for k in range(n_taps)]
        bias = jnp.broadcast_to(cb_ref[:, lanes], (SUBLANES, LANES))

        def block(r, carry, lanes=lanes, taps=taps, bias=bias):
            r0 = pl.multiple_of(r * CONV_ROWS, CONV_ROWS)
            _conv_tile_dense(glu_full, c_ref, r0, lanes, taps, bias)
            return carry

        lax.fori_loop(0, n_blocks, block, 0)


def _mixer_ab_kernel(tt, bb, has_state, start_pos, *refs):
    x_ref, refs = refs[0], refs[1:]
    if has_state:
        (sp_ref, sc_ref), refs = refs[:2], refs[2:]
    (g_ref, win_ref, pw_ref, ps_ref, cw_ref, cb_ref, lg_ref, lb_ref, wout_ref,
     xo_ref, po_ref, co_ref, xa_full, glu_full, c_ref) = refs
    rows = tt * bb
    hp = POOL_BUF * bb
    hc = (cw_ref.shape[0] - 1) * bb
    ti = pl.program_id(1)

    if has_state:
        xa_full[0:hp, :] = sp_ref[...].reshape(hp, xa_full.shape[1])
        glu_full[0:hc, :] = sc_ref[...].reshape(hc, glu_full.shape[1])
    else:
        @pl.when(ti == 0)
        def _():
            xa_full[...] = jnp.zeros(xa_full.shape, F32)
            glu_full[...] = jnp.zeros(glu_full.shape, F32)

        xa_full[0:hp, :] = xa_full[rows:rows + hp, :]
        glu_full[0:hc, :] = glu_full[rows:rows + hc, :]

    x = x_ref[...].reshape(rows, x_ref.shape[2])
    h = _rmsnorm(x, g_ref[...]).astype(BF16)
    glu, ya = _ab_front(h, win_ref, pw_ref, ps_ref, xa_full, bb, start_pos + ti * tt)
    glu_full[hc:hc + rows, :] = glu
    po_ref[...] = xa_full[rows:rows + hp, :].reshape(po_ref.shape)
    co_ref[...] = glu_full[rows:rows + hc, :].reshape(co_ref.shape)

    _depthwise_conv(glu_full, c_ref, rows, bb, cw_ref, cb_ref)
    yb = _silu(_layernorm(c_ref[...], lg_ref[...], lb_ref[...])).astype(BF16)
    out = x + _bdot(jnp.concatenate([ya, yb], axis=1), wout_ref[...])
    xo_ref[...] = out.reshape(xo_ref.shape)


def _mixer_ab_stateful_kernel(tt, bb, start_pos, x_hbm, *refs):
    rest, (x_buf, x_sem), scratch = refs[:14], refs[14:16], refs[16:]
    loads = _step_copies(x_hbm, x_buf, x_sem, pl.program_id(0) * bb, False)
    for c in loads:
        c.start()
    for c in loads:
        c.wait()
    _mixer_ab_kernel(tt, bb, True, start_pos, x_buf, *rest, *scratch)


def _ffn_conv_slabs(full_ref, up, halo_rows, fw, fb, bb, rows):
    halo = halo_rows.shape[0]
    full_ref[0:halo, :] = halo_rows
    full_ref[halo:halo + rows, :] = up
    tail = full_ref[rows:rows + halo, :]
    cv = (full_ref[0:rows, :] * fw[0:1, :] + full_ref[bb:bb + rows, :] * fw[1:2, :]
          + up * fw[2:3, :] + fb)
    return cv, tail


def _ffn_kernel(tt, bb, has_state, final_norm, *refs):
    x_ref, refs = refs[0], refs[1:]
    if has_state:
        st_ref, refs = refs[0], refs[1:]
    (g_ref, wup_ref, fw_ref, fb_ref, wdown_ref, fin_ref,
     xo_ref, so_ref, h_ref, acc_ref, full_a_ref, full_g_ref, carry_ref) = refs
    rows = tt * bb
    d_ff = wdown_ref.shape[0]
    n_keep = fw_ref.shape[0] - 1
    ti = pl.program_id(1)

    d_model = x_ref.shape[2]
    h_ref[...] = _rmsnorm(x_ref[...].reshape(rows, d_model), g_ref[...]).astype(BF16)

    if not has_state:
        @pl.when(ti == 0)
        def _():
            carry_ref[...] = jnp.zeros(carry_ref.shape, F32)

    def up_pair(c):
        return [_bdot(h_ref[...], wup_ref[:, off:off + FF_CHUNK])
                for off in (c * FF_CHUNK, d_ff + c * FF_CHUNK)]

    n_chunks = d_ff // FF_CHUNK
    ups_ahead = [up_pair(c) for c in range(UP_AHEAD)]
    acts = []
    for c in range(n_chunks):
        ups = ups_ahead.pop(0)
        if c + UP_AHEAD < n_chunks:
            ups_ahead.append(up_pair(c + UP_AHEAD))
        col_pair = [slice(off, off + FF_CHUNK) for off in (c * FF_CHUNK, d_ff + c * FF_CHUNK)]
        conv = []
        for full_ref, cols, up in zip((full_a_ref, full_g_ref), col_pair, ups):
            if has_state:
                halo_rows = st_ref[:, :, cols].reshape(n_keep * bb, FF_CHUNK)
            else:
                halo_rows = carry_ref[:, cols]
            cv, tail = _ffn_conv_slabs(full_ref, up, halo_rows, fw_ref[:, cols], fb_ref[:, cols], bb, rows)
            so_ref[:, :, cols] = tail.reshape(n_keep, bb, FF_CHUNK)
            if not has_state:
                carry_ref[:, cols] = tail
            conv.append(cv)
        acts.append((_silu(conv[0]) * conv[1]).astype(BF16))
        if len(acts) == DOWN_GROUP or c + 1 == n_chunks:
            lo = (c + 1 - len(acts)) * FF_CHUNK
            act = acts[0] if len(acts) == 1 else jnp.concatenate(acts, axis=1)
            base = x_ref[...].reshape(rows, d_model) if lo == 0 else acc_ref[...]
            total = base + _bdot(act, wdown_ref[lo:(c + 1) * FF_CHUNK, :])
            if c + 1 < n_chunks:
                acc_ref[...] = total
            else:
                if final_norm:
                    total = _rmsnorm(total, fin_ref[...])
                xo_ref[...] = total.reshape(xo_ref.shape)
            acts = []


def _tile_copies(hbm, buf, sems, slot, step, tt, to_hbm):
    copies = []
    for b in range(buf.shape[2]):
        rows_hbm = hbm.at[b, pl.ds(step * tt, tt), :]
        rows_vmem = buf.at[slot, :, b, :]
        src, dst = (rows_vmem, rows_hbm) if to_hbm else (rows_hbm, rows_vmem)
        copies.append(pltpu.make_async_copy(src, dst, sems.at[slot, b]))
    return copies


def _with_relayout(tt, x_hbm, xo_hbm, xbuf, obuf, in_sem, out_sem, compute):
    s = pl.program_id(1)
    n_steps = pl.num_programs(1)
    slot = s % 2

    def start(copies):
        for c in copies:
            c.start()

    def wait(copies):
        for c in copies:
            c.wait()

    @pl.when(s == 0)
    def _():
        start(_tile_copies(x_hbm, xbuf, in_sem, 0, 0, tt, False))

    @pl.when(s + 1 < n_steps)
    def _():
        start(_tile_copies(x_hbm, xbuf, in_sem, 1 - slot, s + 1, tt, False))

    wait(_tile_copies(x_hbm, xbuf, in_sem, slot, s, tt, False))

    @pl.when(s >= 2)
    def _():
        wait(_tile_copies(xo_hbm, obuf, out_sem, slot, s - 2, tt, True))

    compute(xbuf.at[slot], obuf.at[slot])
    start(_tile_copies(xo_hbm, obuf, out_sem, slot, s, tt, True))

    @pl.when(s == n_steps - 1)
    def _():
        @pl.when(s >= 1)
        def _():
            wait(_tile_copies(xo_hbm, obuf, out_sem, 1 - slot, s - 1, tt, True))
        wait(_tile_copies(xo_hbm, obuf, out_sem, slot, s, tt, True))


def _ffn_relayout_kernel(tt, bb, final_norm, x_hbm, *refs):
    weights, (xo_hbm, so_ref, xbuf, obuf, in_sem, out_sem), scratch = refs[:6], refs[6:12], refs[12:]
    _with_relayout(tt, x_hbm, xo_hbm, xbuf, obuf, in_sem, out_sem,
                   lambda x_tile, out_tile: _ffn_kernel(tt, bb, False, final_norm, x_tile, *weights,
                                                        out_tile, so_ref, *scratch))


def _mixer_ab_relayout_kernel(tt, bb, x_hbm, *refs):
    weights, (xo_hbm, po_ref, co_ref, xbuf, obuf, in_sem, out_sem), scratch = refs[:9], refs[9:16], refs[16:]
    _with_relayout(tt, x_hbm, xo_hbm, xbuf, obuf, in_sem, out_sem,
                   lambda x_tile, out_tile: _mixer_ab_kernel(tt, bb, False, 0, x_tile, *weights,
                                                             out_tile, po_ref, co_ref, *scratch))


def _step_copies(hbm, buf, sems, b0, to_hbm):
    copies = []
    for t in range(buf.shape[0]):
        rows_hbm = hbm.at[pl.ds(b0, buf.shape[1]), t, :]
        src, dst = (buf.at[t], rows_hbm) if to_hbm else (rows_hbm, buf.at[t])
        copies.append(pltpu.make_async_copy(src, dst, sems.at[t]))
    return copies


def _ffn_stateful_kernel(tt, bb, layer, final_norm, *refs):
    x_ref, st_hbm, refs = refs[0], refs[1], refs[2:]
    if layer:
        prev_hbm, refs = refs[0], refs[1:]
    weights, (xo_ref, so_hbm, st_buf, so_buf, st_sem, so_sem), refs = refs[:6], refs[6:12], refs[12:]
    if layer:
        (prev_buf, prev_sem), refs = refs[:2], refs[2:]
    if final_norm:
        (y_buf, y_sem), refs = refs[:2], refs[2:]
    b0 = pl.program_id(0) * bb

    fetch_prev = [c for l in range(layer)
                  for c in _step_copies(prev_hbm.at[l], prev_buf.at[l], prev_sem.at[0, l], b0, False)]
    for c in fetch_prev:
        c.start()
    loads = _step_copies(st_hbm.at[layer], st_buf, st_sem, b0, False)
    for c in loads:
        c.start()
    for c in loads:
        c.wait()
    _ffn_kernel(tt, bb, True, final_norm, x_ref, st_buf, *weights,
                y_buf if final_norm else xo_ref, so_buf, *refs)
    stores = _step_copies(so_hbm.at[layer], so_buf, so_sem, b0, True)
    for c in fetch_prev:
        c.wait()
    for l in range(layer):
        stores += _step_copies(so_hbm.at[l], prev_buf.at[l], prev_sem.at[1, l], b0, True)
    if final_norm:
        stores += _step_copies(xo_ref, y_buf, y_sem, b0, True)
    for c in stores:
        c.start()
    for c in stores:
        c.wait()


def _mixer_c_kernel(tt, bb, *refs):
    (x_ref, g_ref, win_ref, lg_ref, lb_ref, sw_ref, sb_ref, wout_ref,
     xo_ref, vo_ref, v_ref, vb_ref, z_ref) = refs
    rows = tt * bb
    d_c = wout_ref.shape[0]

    x = x_ref[...].reshape(rows, x_ref.shape[2])
    h = _rmsnorm(x, g_ref[...]).astype(BF16)
    v = _layernorm(_bdot(h, win_ref[:, d_c:]), lg_ref[...], lb_ref[...])
    u = _bdot(h, win_ref[:, :d_c])

    if bb == 1:
        n_heads = sw_ref.shape[0]
        dh = d_c // n_heads
        n_chunks = tt // CHUNK
        vo_ref[0] = v[rows - CHUNK:, :]
        vb_ref[...] = v.astype(BF16)
        row_i = lax.broadcasted_iota(jnp.int32, (CHUNK, CHUNK), 0)
        col_i = lax.broadcasted_iota(jnp.int32, (CHUNK, CHUNK), 1)
        for hd in range(n_heads):
            hcols = slice(hd * dh, (hd + 1) * dh)
            m = jnp.where(col_i <= row_i, sw_ref[hd], 0.0).astype(BF16)
            rhs = jnp.concatenate(
                [vb_ref[c * CHUNK:(c + 1) * CHUNK, hcols] for c in range(n_chunks)], axis=1)
            zc = _bdot(m, rhs)
            bias = sb_ref[:, hcols]
            for c in range(n_chunks):
                z_ref[c * CHUNK:(c + 1) * CHUNK, hcols] = zc[:, c * dh:(c + 1) * dh] + bias
    else:
        vo_ref[...] = v.reshape(vo_ref.shape)
        v_ref[...] = v
        for i in range(tt):
            zi = jnp.broadcast_to(sb_ref[i:i + 1, :], (bb, d_c))
            for j in range(i + 1):
                zi = zi + v_ref[j * bb:(j + 1) * bb, :] * sw_ref[i * tt + j:i * tt + j + 1, :]
            z_ref[i * bb:(i + 1) * bb, :] = zi

    gated = (u * z_ref[...]).astype(BF16)
    out = x + _bdot(gated, wout_ref[...])
    xo_ref[...] = out.reshape(xo_ref.shape)


def _mixer_c_short_kernel(tt, bb, *refs):
    ins, (xo_ref, v_hbm, v_buf, v_sem), scratch = refs[:8], refs[8:12], refs[12:]
    _mixer_c_kernel(tt, bb, *ins, xo_ref, v_buf, *scratch)
    stores = _step_copies(v_hbm, v_buf, v_sem, pl.program_id(0) * bb, True)
    for c in stores:
        c.start()
    for c in stores:
        c.wait()


def _layer_shape(entry):
    return entry[0].shape[1:] if isinstance(entry, tuple) else entry.shape


def _resident(entries):
    arrays, specs = [], []
    for entry in entries:
        arr, layer = entry if isinstance(entry, tuple) else (entry, None)
        if layer is None:
            block, index = arr.shape, (0,) * arr.ndim
        else:
            block, index = (None,) + arr.shape[1:], (layer,) + (0,) * (arr.ndim - 1)
        arrays.append(arr)
        specs.append(pl.BlockSpec(block, lambda bi, ti, index=index: index,
                                  pipeline_mode=pl.Buffered(1)))
    return arrays, specs


def _tm_spec(tt, bb, width):
    return pl.BlockSpec((tt, bb, width), lambda bi, ti: (ti, bi, 0))


def _bm_spec(bb, tt, width):
    return pl.BlockSpec((bb, tt, width), lambda bi, ti: (bi, ti, 0))


def _tm_state_spec(n, bb, width):
    return pl.BlockSpec((n, bb, width), lambda bi, ti: (0, bi, 0))


_PARAMS = pltpu.CompilerParams(
    dimension_semantics=("arbitrary", "arbitrary"), vmem_limit_bytes=VMEM_LIMIT)


def _ab_weights(w):
    return (w["mix_norm"], w["w_in_ab"], w["pool_w"], w["pool_scale"], w["conv_w"], w["conv_b"],
            w["conv_norm_g"], w["conv_norm_b"], w["w_out_ab"])


def _ab_dims(w):
    return (_layer_shape(w["pool_scale"])[1], _layer_shape(w["conv_b"])[1], _layer_shape(w["conv_w"])[0])


def _ab_scratch(tt, bb, d_a, d_b, conv_k):
    return [pltpu.VMEM(((tt + POOL_BUF) * bb, d_a), F32),
            pltpu.VMEM(((tt + conv_k - 1) * bb, d_b), F32),
            pltpu.VMEM((tt * bb, d_b), F32)]


def _relayout_scratch(tt, batch, d_model):
    return [pltpu.VMEM((2, tt, batch, d_model), F32), pltpu.VMEM((2, tt, batch, d_model), F32),
            pltpu.SemaphoreType.DMA((2, batch)), pltpu.SemaphoreType.DMA((2, batch))]


def _mixer_ab_stateful(x, states, w, bb, start_pos, name):
    batch, tt, d_model = x.shape
    d_a, d_b, conv_k = _ab_dims(w)
    state_specs = [_tm_state_spec(POOL_BUF, bb, d_a), _tm_state_spec(conv_k - 1, bb, d_b)]
    weights, weight_specs = _resident(_ab_weights(w))
    return pl.pallas_call(
        functools.partial(_mixer_ab_stateful_kernel, tt, bb, start_pos),
        grid=(batch // bb, 1),
        in_specs=[pl.BlockSpec(memory_space=pl.ANY)] + state_specs + weight_specs,
        out_specs=[_tm_spec(tt, bb, d_model)] + state_specs,
        out_shape=[jax.ShapeDtypeStruct((tt, batch, d_model), F32),
                   jax.ShapeDtypeStruct((POOL_BUF, batch, d_a), F32),
                   jax.ShapeDtypeStruct((conv_k - 1, batch, d_b), F32)],
        scratch_shapes=([pltpu.VMEM((tt, bb, d_model), F32), pltpu.SemaphoreType.DMA((tt,))]
                        + _ab_scratch(tt, bb, d_a, d_b, conv_k)),
        compiler_params=_PARAMS,
        name=name,
    )(x, *states, *weights)


def _mixer_ab_relayout(x, w, tt, name):
    batch, t_len, d_model = x.shape
    d_a, d_b, conv_k = _ab_dims(w)
    assert batch == SUBLANES and t_len % tt == 0 and tt >= conv_k - 1
    weights, weight_specs = _resident(_ab_weights(w))
    return pl.pallas_call(
        functools.partial(_mixer_ab_relayout_kernel, tt, batch),
        grid=(1, t_len // tt),
        in_specs=[pl.BlockSpec(memory_space=pl.ANY)] + weight_specs,
        out_specs=[pl.BlockSpec(memory_space=pl.ANY), _tm_state_spec(POOL_BUF, batch, d_a),
                   _tm_state_spec(conv_k - 1, batch, d_b)],
        out_shape=[jax.ShapeDtypeStruct(x.shape, F32),
                   jax.ShapeDtypeStruct((POOL_BUF, batch, d_a), F32),
                   jax.ShapeDtypeStruct((conv_k - 1, batch, d_b), F32)],
        scratch_shapes=_relayout_scratch(tt, batch, d_model) + _ab_scratch(tt, batch, d_a, d_b, conv_k),
        compiler_params=_PARAMS,
        name=name,
    )(x, *weights)


def _ffn_scratch(tt, bb, d_model, d_ff, n_keep):
    rows = tt * bb
    return [pltpu.VMEM((rows, d_model), BF16),
            pltpu.VMEM((rows, d_model), F32),
            pltpu.VMEM((rows + n_keep * bb, FF_CHUNK), F32),
            pltpu.VMEM((rows + n_keep * bb, FF_CHUNK), F32),
            pltpu.VMEM((n_keep * bb, 2 * d_ff), F32)]


def _ffn_weights(w):
    return (w["ffn_norm"], w["w_up"], w["ffn_conv_w"], w["ffn_conv_b"], w["w_down"], w["final_norm"])


def _ffn_stateful(x, states, prev_out, w, layer, bb, final_norm, name):
    tt, batch, d_model = x.shape
    d_ff = _layer_shape(w["w_down"])[0]
    n_keep = _layer_shape(w["ffn_conv_w"])[0] - 1
    assert (prev_out is None) == (layer == 0)
    hbm = pl.BlockSpec(memory_space=pl.ANY)
    weights, weight_specs = _resident(_ffn_weights(w))
    state_bufs = [pltpu.VMEM((n_keep, bb, 2 * d_ff), F32), pltpu.VMEM((n_keep, bb, 2 * d_ff), F32),
                  pltpu.SemaphoreType.DMA((n_keep,)), pltpu.SemaphoreType.DMA((n_keep,))]
    prev_bufs = ([pltpu.VMEM((layer, n_keep, bb, 2 * d_ff), F32),
                  pltpu.SemaphoreType.DMA((2, layer, n_keep))] if layer else [])
    y_bufs = [pltpu.VMEM((tt, bb, d_model), F32), pltpu.SemaphoreType.DMA((tt,))] if final_norm else []
    x_out_spec = hbm if final_norm else _tm_spec(tt, bb, d_model)
    x_out_shape = (batch, tt, d_model) if final_norm else x.shape
    return pl.pallas_call(
        functools.partial(_ffn_stateful_kernel, tt, bb, layer, final_norm),
        grid=(batch // bb, 1),
        in_specs=[_tm_spec(tt, bb, d_model), hbm] + ([hbm] if layer else []) + weight_specs,
        out_specs=[x_out_spec, hbm],
        out_shape=[jax.ShapeDtypeStruct(x_out_shape, F32),
                   jax.ShapeDtypeStruct((layer + 1, batch, n_keep, 2 * d_ff), F32)],
        scratch_shapes=state_bufs + prev_bufs + y_bufs + _ffn_scratch(tt, bb, d_model, d_ff, n_keep),
        compiler_params=_PARAMS,
        name=name,
    )(x, states, *(() if prev_out is None else (prev_out,)), *weights)


def _ffn_relayout(x, w, tt, final_norm, name):
    batch, t_len, d_model = x.shape
    d_ff = _layer_shape(w["w_down"])[0]
    n_keep = _layer_shape(w["ffn_conv_w"])[0] - 1
    assert batch == SUBLANES and t_len % tt == 0
    weights, weight_specs = _resident(_ffn_weights(w))
    return pl.pallas_call(
        functools.partial(_ffn_relayout_kernel, tt, batch, final_norm),
        grid=(1, t_len // tt),
        in_specs=[pl.BlockSpec(memory_space=pl.ANY)] + weight_specs,
        out_specs=[pl.BlockSpec(memory_space=pl.ANY), _tm_state_spec(n_keep, batch, 2 * d_ff)],
        out_shape=[jax.ShapeDtypeStruct(x.shape, F32),
                   jax.ShapeDtypeStruct((n_keep, batch, 2 * d_ff), F32)],
        scratch_shapes=(_relayout_scratch(tt, batch, d_model)
                        + _ffn_scratch(tt, batch, d_model, d_ff, n_keep)),
        compiler_params=_PARAMS,
        name=name,
    )(x, *weights)


def _mixer_c(x, w, tt, bb, name):
    d_c = _layer_shape(w["w_out_c"])[0]
    extra_scratch = []
    if bb == 1:
        assert tt % CHUNK == 0
        batch, t_len, d_model = x.shape
        body = _mixer_c_kernel
        x_spec = _bm_spec(1, tt, d_model)
        v_spec = pl.BlockSpec((1, CHUNK, d_c), lambda bi, ti: (bi, 0, 0))
        v_shape = (batch, CHUNK, d_c)
        sgu = (w["sgu_w"], w["sgu_b_chunk"])
    else:
        t_len, batch, d_model = x.shape
        assert t_len == tt
        body = _mixer_c_short_kernel
        x_spec = _tm_spec(tt, bb, d_model)
        v_spec = pl.BlockSpec(memory_space=pl.ANY)
        v_shape = (batch, t_len, d_c)
        sgu = (w["sgu_w_short"], w["sgu_b_short"])
        extra_scratch = [pltpu.VMEM((tt, bb, d_c), F32), pltpu.SemaphoreType.DMA((tt,))]
    rows = tt * bb
    weights, weight_specs = _resident(
        (w["mix_norm"], w["w_in_c"], w["sgu_norm_g"], w["sgu_norm_b"]) + sgu + (w["w_out_c"],))
    return pl.pallas_call(
        functools.partial(body, tt, bb),
        grid=(batch // bb, t_len // tt),
        in_specs=[x_spec] + weight_specs,
        out_specs=[x_spec, v_spec],
        out_shape=[jax.ShapeDtypeStruct(x.shape, F32), jax.ShapeDtypeStruct(v_shape, F32)],
        scratch_shapes=extra_scratch + [pltpu.VMEM((rows, d_c), F32),
                                        pltpu.VMEM((rows, d_c), BF16),
                                        pltpu.VMEM((rows, d_c), F32)],
        compiler_params=_PARAMS,
        name=name,
    )(x, *weights)


def _tm(a):
    return jnp.swapaxes(a, 0, 1)


def kernel(x_prompt, x_sample, state_pool, state_conv, state_ffn, mix_norm, w_in_ab, pool_w, pool_scale, conv_w, conv_b, conv_norm_g, conv_norm_b, w_out_ab, w_in_c, sgu_norm_g, sgu_norm_b, sgu_w, sgu_bias, w_out_c, ffn_norm, w_up, ffn_conv_w, ffn_conv_b, w_down, final_norm):
    dec_seq = x_sample.shape[1]
    n_heads = sgu_w.shape[1]
    d_c = w_out_c.shape[1]
    dh = d_c // n_heads
    row = lambda a: a.reshape(1, -1)
    rows_of = lambda a: a.reshape(a.shape[0], 1, -1)
    fin = row(final_norm)
    mix_n, ffn_n, ffn_cb = rows_of(mix_norm), rows_of(ffn_norm), rows_of(ffn_conv_b)
    w_up_b, w_down_b = w_up.astype(BF16), w_down.astype(BF16)
    ffn = lambda layer: dict(ffn_norm=(ffn_n, layer), w_up=(w_up_b, layer), ffn_conv_w=(ffn_conv_w, layer),
                             ffn_conv_b=(ffn_cb, layer), w_down=(w_down_b, layer), final_norm=fin)
    l0 = dict(mix_norm=(mix_n, 0), w_in_ab=w_in_ab[0].astype(BF16), pool_w=pool_w[0].astype(BF16),
              pool_scale=row(pool_scale[0]), conv_w=conv_w[0], conv_b=row(conv_b[0]),
              conv_norm_g=row(conv_norm_g[0]), conv_norm_b=row(conv_norm_b[0]),
              w_out_ab=w_out_ab[0].astype(BF16), **ffn(0))
    sw_short = jnp.repeat(jnp.transpose(sgu_w[0][:, :dec_seq, :dec_seq], (1, 2, 0)), dh, axis=2)
    l1 = dict(mix_norm=(mix_n, 1), w_in_c=w_in_c[0].astype(BF16),
              sgu_norm_g=row(sgu_norm_g[0]), sgu_norm_b=row(sgu_norm_b[0]),
              sgu_w=sgu_w[0], sgu_b_chunk=jnp.repeat(sgu_bias[0].T, dh, axis=1),
              sgu_w_short=sw_short.reshape(dec_seq * dec_seq, d_c),
              sgu_b_short=jnp.repeat(sgu_bias[0][:, :dec_seq].T, dh, axis=1),
              w_out_c=w_out_c[0].astype(BF16), **ffn(1))

    xp, pool_p, conv_p = _mixer_ab_relayout(x_prompt, l0, 128, "mixer_ab_prompt")
    xp, f0_p = _ffn_relayout(xp, l0, 64, False, "ffn0_prompt")
    xp, v_p = _mixer_c(xp, l1, 512, 1, "mixer_c_prompt")
    yp, f1_p = _ffn_relayout(xp, l1, 64, True, "ffn1_prompt")

    bb_s = 64
    xs, pool_s, conv_s = _mixer_ab_stateful(x_sample, (_tm(state_pool[0]), _tm(state_conv[0])),
                                            l0, bb_s, PAST_LEN, "mixer_ab_sample")
    xs, ffn_s = _ffn_stateful(xs, state_ffn, None, l0, 0, bb_s, False, "ffn0_sample")
    xs, v_s = _mixer_c(xs, l1, dec_seq, bb_s, "mixer_c_sample")
    ys, ffn_s = _ffn_stateful(xs, state_ffn, ffn_s, l1, 1, bb_s, True, "ffn1_sample")

    return (yp, ys, _tm(pool_p)[None], _tm(pool_s)[None], _tm(conv_p)[None], _tm(conv_s)[None],
            v_p[None], v_s[None],
            jnp.stack([_tm(f0_p), _tm(f1_p)]), ffn_s)
```

```python
import functools

import jax
import jax.numpy as jnp
from jax import lax
from jax.experimental import pallas as pl
from jax.experimental.pallas import tpu as pltpu

F32 = jnp.float32
BF16 = jnp.bfloat16

EPS = 1e-6
POOL_WINDOWS = (2, 4, 8, 16)
POOL_BUF = max(POOL_WINDOWS) - 1
CHUNK = 128
PAST_LEN = 16384
LANES = 128
SUBLANES = 8
FF_CHUNK = 256
DOWN_GROUP = 4
UP_AHEAD = 3
CONV_ROWS = 64
VMEM_LIMIT = 56 * 1024 * 1024


def _rmsnorm(x, g):
    ms = jnp.mean(x * x, axis=-1, keepdims=True)
    return x * lax.rsqrt(ms + EPS) * g


def _layernorm(x, g, b):
    mu = jnp.mean(x, axis=-1, keepdims=True)
    xc = x - mu
    var = jnp.mean(xc * xc, axis=-1, keepdims=True)
    return xc * lax.rsqrt(var + EPS) * g + b


def _silu(x):
    return x * jax.nn.sigmoid(x)


def _bdot(a, b):
    return jnp.dot(a, b, preferred_element_type=F32)


def _ab_front(h, win_ref, pw_ref, ps_ref, xa_full, bb, pos0):
    rows = h.shape[0]
    d_a = xa_full.shape[1]
    hp = POOL_BUF * bb
    proj = _bdot(h, win_ref[...])
    d_b = (proj.shape[1] - d_a) // 2
    xa = proj[:, :d_a]
    glu = proj[:, d_a:d_a + d_b] * jax.nn.sigmoid(proj[:, d_a + d_b:])
    xa_full[hp:hp + rows, :] = xa
    pos = pos0 + lax.broadcasted_iota(jnp.int32, (rows, LANES), 0) // bb
    group = d_a // len(POOL_WINDOWS)
    ya = []
    for gi, w in enumerate(POOL_WINDOWS):
        c0 = gi * group
        xg = xa[:, c0:c0 + group]
        win = xg
        for k in range(1, w):
            win = win + xa_full[hp - k * bb:hp - k * bb + rows, c0:c0 + group]
        cnt = jnp.minimum(pos + 1, w).astype(F32)
        d = (win / cnt - xg).astype(BF16)
        ya.append((_bdot(d, pw_ref[gi]) * ps_ref[:, c0:c0 + group]).astype(BF16))
    return glu, jnp.concatenate(ya, axis=1)


def _conv_tile_dense(glu_full, c_ref, r0, lanes, taps, bias):
    n_out = CONV_ROWS // SUBLANES
    acc = [bias] * n_out
    for m in range(n_out + len(taps) - 1):
        grp = glu_full[pl.ds(r0 + m * SUBLANES, SUBLANES), lanes]
        for k in range(len(taps)):
            if 0 <= m - k < n_out:
                acc[m - k] = acc[m - k] + grp * taps[k]
    for i in range(n_out):
        c_ref[pl.ds(r0 + i * SUBLANES, SUBLANES), lanes] = acc[i]


def _depthwise_conv(glu_full, c_ref, rows, bb, cw_ref, cb_ref):
    n_taps = cw_ref.shape[0]
    n_blocks = rows // CONV_ROWS
    if bb != SUBLANES:
        def block(r, carry):
            r0 = pl.multiple_of(r * CONV_ROWS, CONV_ROWS)
            acc = jnp.broadcast_to(cb_ref[...], (CONV_ROWS, c_ref.shape[1]))
            for k in range(n_taps):
                acc = acc + glu_full[pl.ds(r0 + k * bb, CONV_ROWS), :] * cw_ref[k:k + 1, :]
            c_ref[pl.ds(r0, CONV_ROWS), :] = acc
            return carry

        lax.fori_loop(0, n_blocks, block, 0)
        return
    for lane0 in range(0, c_ref.shape[1], LANES):
        lanes = slice(lane0, lane0 + LANES)
        taps = [jnp.broadcast_to(cw_ref[k:k + 1, lanes], (SUBLANES, LANES)) for k in range(n_taps)]
        bias = jnp.broadcast_to(cb_ref[:, lanes], (SUBLANES, LANES))

        def block(r, carry, lanes=lanes, taps=taps, bias=bias):
            r0 = pl.multiple_of(r * CONV_ROWS, CONV_ROWS)
            _conv_tile_dense(glu_full, c_ref, r0, lanes, taps, bias)
            return carry

        lax.fori_loop(0, n_blocks, block, 0)


def _mixer_ab_kernel(tt, bb, has_state, start_pos, *refs):
    x_ref, refs = refs[0], refs[1:]
    if has_state:
        (sp_ref, sc_ref), refs = refs[:2], refs[2:]
    (g_ref, win_ref, pw_ref, ps_ref, cw_ref, cb_ref, lg_ref, lb_ref, wout_ref,
     xo_ref, po_ref, co_ref, xa_full, glu_full, c_ref) = refs
    rows = tt * bb
    hp = POOL_BUF * bb
    hc = (cw_ref.shape[0] - 1) * bb
    ti = pl.program_id(1)

    if has_state:
        xa_full[0:hp, :] = sp_ref[...].reshape(hp, xa_full.shape[1])
        glu_full[0:hc, :] = sc_ref[...].reshape(hc, glu_full.shape[1])
    else:
        @pl.when(ti == 0)
        def _():
            xa_full[...] = jnp.zeros(xa_full.shape, F32)
            glu_full[...] = jnp.zeros(glu_full.shape, F32)

        xa_full[0:hp, :] = xa_full[rows:rows + hp, :]
        glu_full[0:hc, :] = glu_full[rows:rows + hc, :]

    x = x_ref[...].reshape(rows, x_ref.shape[2])
    h = _rmsnorm(x, g_ref[...]).astype(BF16)
    glu, ya = _ab_front(h, win_ref, pw_ref, ps_ref, xa_full, bb, start_pos + ti * tt)
    glu_full[hc:hc + rows, :] = glu
    po_ref[...] = xa_full[rows:rows + hp, :].reshape(po_ref.shape)
    co_ref[...] = glu_full[rows:rows + hc, :].reshape(co_ref.shape)

    _depthwise_conv(glu_full, c_ref, rows, bb, cw_ref, cb_ref)
    yb = _silu(_layernorm(c_ref[...], lg_ref[...], lb_ref[...])).astype(BF16)
    out = x + _bdot(jnp.concatenate([ya, yb], axis=1), wout_ref[...])
    xo_ref[...] = out.reshape(xo_ref.shape)


def _mixer_ab_stateful_kernel(tt, bb, start_pos, x_hbm, *refs):
    rest, (x_buf, x_sem), scratch = refs[:14], refs[14:16], refs[16:]
    loads = _step_copies(x_hbm, x_buf, x_sem, pl.program_id(0) * bb, False)
    for c in loads:
        c.start()
    for c in loads:
        c.wait()
    _mixer_ab_kernel(tt, bb, True, start_pos, x_buf, *rest, *scratch)


def _ffn_conv_slabs(full_ref, up, halo_rows, fw, fb, bb, rows):
    halo = halo_rows.shape[0]
    full_ref[0:halo, :] = halo_rows
    full_ref[halo:halo + rows, :] = up
    tail = full_ref[rows:rows + halo, :]
    cv = (full_ref[0:rows, :] * fw[0:1, :] + full_ref[bb:bb + rows, :] * fw[1:2, :]
          + up * fw[2:3, :] + fb)
    return cv, tail


def _ffn_kernel(tt, bb, has_state, final_norm, *refs):
    x_ref, refs = refs[0], refs[1:]
    if has_state:
        st_ref, refs = refs[0], refs[1:]
    (g_ref, wup_ref, fw_ref, fb_ref, wdown_ref, fin_ref,
     xo_ref, so_ref, h_ref, acc_ref, full_a_ref, full_g_ref, carry_ref) = refs
    rows = tt * bb
    d_ff = wdown_ref.shape[0]
    n_keep = fw_ref.shape[0] - 1
    ti = pl.program_id(1)

    d_model = x_ref.shape[2]
    h_ref[...] = _rmsnorm(x_ref[...].reshape(rows, d_model), g_ref[...]).astype(BF16)

    if not has_state:
        @pl.when(ti == 0)
        def _():
            carry_ref[...] = jnp.zeros(carry_ref.shape, F32)

    def up_pair(c):
        return [_bdot(h_ref[...], wup_ref[:, off:off + FF_CHUNK])
                for off in (c * FF_CHUNK, d_ff + c * FF_CHUNK)]

    n_chunks = d_ff // FF_CHUNK
    ups_ahead = [up_pair(c) for c in range(UP_AHEAD)]
    acts = []
    for c in range(n_chunks):
        ups = ups_ahead.pop(0)
        if c + UP_AHEAD < n_chunks:
            ups_ahead.append(up_pair(c + UP_AHEAD))
        col_pair = [slice(off, off + FF_CHUNK) for off in (c * FF_CHUNK, d_ff + c * FF_CHUNK)]
        conv = []
        for full_ref, cols, up in zip((full_a_ref, full_g_ref), col_pair, ups):
            if has_state:
                halo_rows = st_ref[:, :, cols].reshape(n_keep * bb, FF_CHUNK)
            else:
                halo_rows = carry_ref[:, cols]
            cv, tail = _ffn_conv_slabs(full_ref, up, halo_rows, fw_ref[:, cols], fb_ref[:, cols], bb, rows)
            so_ref[:, :, cols] = tail.reshape(n_keep, bb, FF_CHUNK)
            if not has_state:
                carry_ref[:, cols] = tail
            conv.append(cv)
        acts.append((_silu(conv[0]) * conv[1]).astype(BF16))
        if len(acts) == DOWN_GROUP or c + 1 == n_chunks:
            lo = (c + 1 - len(acts)) * FF_CHUNK
            act = acts[0] if len(acts) == 1 else jnp.concatenate(acts, axis=1)
            base = x_ref[...].reshape(rows, d_model) if lo == 0 else acc_ref[...]
            total = base + _bdot(act, wdown_ref[lo:(c + 1) * FF_CHUNK, :])
            if c + 1 < n_chunks:
                acc_ref[...] = total
            else:
                if final_norm:
                    total = _rmsnorm(total, fin_ref[...])
                xo_ref[...] = total.reshape(xo_ref.shape)
            acts = []


def _tile_copies(hbm, buf, sems, slot, step, tt, to_hbm):
    copies = []
    for b in range(buf.shape[2]):
        rows_hbm = hbm.at[b, pl.ds(step * tt, tt), :]
        rows_vmem = buf.at[slot, :, b, :]
        src, dst = (rows_vmem, rows_hbm) if to_hbm else (rows_hbm, rows_vmem)
        copies.append(pltpu.make_async_copy(src, dst, sems.at[slot, b]))
    return copies


def _with_relayout(tt, x_hbm, xo_hbm, xbuf, obuf, in_sem, out_sem, compute):
    s = pl.program_id(1)
    n_steps = pl.num_programs(1)
    slot = s % 2

    def start(copies):
        for c in copies:
            c.start()

    def wait(copies):
        for c in copies:
            c.wait()

    @pl.when(s == 0)
    def _():
        start(_tile_copies(x_hbm, xbuf, in_sem, 0, 0, tt, False))

    @pl.when(s + 1 < n_steps)
    def _():
        start(_tile_copies(x_hbm, xbuf, in_sem, 1 - slot, s + 1, tt, False))

    wait(_tile_copies(x_hbm, xbuf, in_sem, slot, s, tt, False))

    @pl.when(s >= 2)
    def _():
        wait(_tile_copies(xo_hbm, obuf, out_sem, slot, s - 2, tt, True))

    compute(xbuf.at[slot], obuf.at[slot])
    start(_tile_copies(xo_hbm, obuf, out_sem, slot, s, tt, True))

    @pl.when(s == n_steps - 1)
    def _():
        @pl.when(s >= 1)
        def _():
            wait(_tile_copies(xo_hbm, obuf, out_sem, 1 - slot, s - 1, tt, True))
        wait(_tile_copies(xo_hbm, obuf, out_sem, slot, s, tt, True))


def _cast_weight(w_hbm, w_vmem, stage, sems, slices):
    copies = [pltpu.make_async_copy(w_hbm.at[sl], stage.at[i % 2], sems.at[i % 2])
              for i, sl in enumerate(slices)]
    copies[0].start()
    for i, sl in enumerate(slices):
        if i + 1 < len(slices):
            copies[i + 1].start()
        copies[i].wait()
        w_vmem[sl] = stage[i % 2].astype(BF16)


def _ffn_relayout_kernel(tt, bb, layer, final_norm, x_hbm, wup_hbm, wdown_hbm, *refs):
    (g_ref, fw_ref, fb_ref, fin_ref), refs = refs[:4], refs[4:]
    (xo_hbm, so_ref, wup_out, wdown_out, xbuf, obuf, in_sem, out_sem,
     wup_ref, wdown_ref, up_stage, down_stage, w_sem), scratch = refs[:13], refs[13:]
    s = pl.program_id(1)
    emit = [pltpu.make_async_copy(wup_ref, wup_out, w_sem.at[2]),
            pltpu.make_async_copy(wdown_ref, wdown_out, w_sem.at[3])]

    @pl.when(s == 0)
    def _():
        d_model, two_ff = wup_ref.shape
        _cast_weight(wup_hbm.at[layer], wup_ref, up_stage, w_sem,
                     [(slice(None), pl.ds(c, FF_CHUNK)) for c in range(0, two_ff, FF_CHUNK)])
        _cast_weight(wdown_hbm.at[layer], wdown_ref, down_stage, w_sem,
                     [(pl.ds(r, FF_CHUNK), slice(None)) for r in range(0, two_ff // 2, FF_CHUNK)])
        for c in emit:
            c.start()

    weights = (g_ref, wup_ref, fw_ref, fb_ref, wdown_ref, fin_ref)
    _with_relayout(tt, x_hbm, xo_hbm, xbuf, obuf, in_sem, out_sem,
                   lambda x_tile, out_tile: _ffn_kernel(tt, bb, False, final_norm, x_tile, *weights,
                                                        out_tile, so_ref, *scratch))

    @pl.when(s == pl.num_programs(1) - 1)
    def _():
        for c in emit:
            c.wait()


def _mixer_ab_relayout_kernel(tt, bb, x_hbm, *refs):
    weights, (xo_hbm, po_ref, co_ref, xbuf, obuf, in_sem, out_sem), scratch = refs[:9], refs[9:16], refs[16:]
    _with_relayout(tt, x_hbm, xo_hbm, xbuf, obuf, in_sem, out_sem,
                   lambda x_tile, out_tile: _mixer_ab_kernel(tt, bb, False, 0, x_tile, *weights,
                                                             out_tile, po_ref, co_ref, *scratch))


def _step_copies(hbm, buf, sems, b0, to_hbm):
    copies = []
    for t in range(buf.shape[0]):
        rows_hbm = hbm.at[pl.ds(b0, buf.shape[1]), t, :]
        src, dst = (buf.at[t], rows_hbm) if to_hbm else (rows_hbm, buf.at[t])
        copies.append(pltpu.make_async_copy(src, dst, sems.at[t]))
    return copies


def _ffn_stateful_kernel(tt, bb, layer, final_norm, *refs):
    x_ref, st_hbm, refs = refs[0], refs[1], refs[2:]
    weights, (xo_ref, so_hbm, st_buf, so_buf, st_sem, so_sem), refs = refs[:6], refs[6:12], refs[12:]
    if final_norm:
        (y_buf, y_sem), refs = refs[:2], refs[2:]
    b0 = pl.program_id(0) * bb

    loads = _step_copies(st_hbm.at[layer], st_buf, st_sem, b0, False)
    for c in loads:
        c.start()
    for c in loads:
        c.wait()
    _ffn_kernel(tt, bb, True, final_norm, x_ref, st_buf, *weights,
                y_buf if final_norm else xo_ref, so_buf, *refs)
    stores = _step_copies(so_hbm, so_buf, so_sem, b0, True)
    if final_norm:
        stores += _step_copies(xo_ref, y_buf, y_sem, b0, True)
    for c in stores:
        c.start()
    for c in stores:
        c.wait()


def _mixer_c_kernel(tt, bb, *refs):
    (x_ref, g_ref, win_ref, lg_ref, lb_ref, sw_ref, sb_ref, wout_ref,
     xo_ref, vo_ref, v_ref, vb_ref, z_ref) = refs
    rows = tt * bb
    d_c = wout_ref.shape[0]

    x = x_ref[...].reshape(rows, x_ref.shape[2])
    h = _rmsnorm(x, g_ref[...]).astype(BF16)
    v = _layernorm(_bdot(h, win_ref[:, d_c:]), lg_ref[...], lb_ref[...])
    u = _bdot(h, win_ref[:, :d_c])

    if bb == 1:
        n_heads = sw_ref.shape[0]
        dh = d_c // n_heads
        n_chunks = tt // CHUNK
        vo_ref[0] = v[rows - CHUNK:, :]
        vb_ref[...] = v.astype(BF16)
        row_i = lax.broadcasted_iota(jnp.int32, (CHUNK, CHUNK), 0)
        col_i = lax.broadcasted_iota(jnp.int32, (CHUNK, CHUNK), 1)
        for hd in range(n_heads):
            hcols = slice(hd * dh, (hd + 1) * dh)
            m = jnp.where(col_i <= row_i, sw_ref[hd], 0.0).astype(BF16)
            rhs = jnp.concatenate(
                [vb_ref[c * CHUNK:(c + 1) * CHUNK, hcols] for c in range(n_chunks)], axis=1)
            zc = _bdot(m, rhs)
            bias = sb_ref[:, hcols]
            for c in range(n_chunks):
                z_ref[c * CHUNK:(c + 1) * CHUNK, hcols] = zc[:, c * dh:(c + 1) * dh] + bias
    else:
        vo_ref[...] = v.reshape(vo_ref.shape)
        v_ref[...] = v
        for i in range(tt):
            zi = jnp.broadcast_to(sb_ref[i:i + 1, :], (bb, d_c))
            for j in range(i + 1):
                zi = zi + v_ref[j * bb:(j + 1) * bb, :] * sw_ref[i * tt + j:i * tt + j + 1, :]
            z_ref[i * bb:(i + 1) * bb, :] = zi

    gated = (u * z_ref[...]).astype(BF16)
    out = x + _bdot(gated, wout_ref[...])
    xo_ref[...] = out.reshape(xo_ref.shape)


def _mixer_c_short_kernel(tt, bb, *refs):
    ins, (xo_ref, v_hbm, v_buf, v_sem), scratch = refs[:8], refs[8:12], refs[12:]
    _mixer_c_kernel(tt, bb, *ins, xo_ref, v_buf, *scratch)
    stores = _step_copies(v_hbm, v_buf, v_sem, pl.program_id(0) * bb, True)
    for c in stores:
        c.start()
    for c in stores:
        c.wait()


def _layer_shape(entry):
    return entry[0].shape[1:] if isinstance(entry, tuple) else entry.shape


def _resident(entries):
    arrays, specs = [], []
    for entry in entries:
        arr, layer = entry if isinstance(entry, tuple) else (entry, None)
        if layer is None:
            block, index = arr.shape, (0,) * arr.ndim
        else:
            block, index = (None,) + arr.shape[1:], (layer,) + (0,) * (arr.ndim - 1)
        arrays.append(arr)
        specs.append(pl.BlockSpec(block, lambda bi, ti, index=index: index,
                                  pipeline_mode=pl.Buffered(1)))
    return arrays, specs


def _tm_spec(tt, bb, width):
    return pl.BlockSpec((tt, bb, width), lambda bi, ti: (ti, bi, 0))


def _bm_spec(bb, tt, width):
    return pl.BlockSpec((bb, tt, width), lambda bi, ti: (bi, ti, 0))


def _tm_state_spec(n, bb, width):
    return pl.BlockSpec((n, bb, width), lambda bi, ti: (0, bi, 0))


_PARAMS = pltpu.CompilerParams(
    dimension_semantics=("arbitrary", "arbitrary"), vmem_limit_bytes=VMEM_LIMIT)


def _ab_weights(w):
    return (w["mix_norm"], w["w_in_ab"], w["pool_w"], w["pool_scale"], w["conv_w"], w["conv_b"],
            w["conv_norm_g"], w["conv_norm_b"], w["w_out_ab"])


def _ab_dims(w):
    return (_layer_shape(w["pool_scale"])[1], _layer_shape(w["conv_b"])[1], _layer_shape(w["conv_w"])[0])


def _ab_scratch(tt, bb, d_a, d_b, conv_k):
    return [pltpu.VMEM(((tt + POOL_BUF) * bb, d_a), F32),
            pltpu.VMEM(((tt + conv_k - 1) * bb, d_b), F32),
            pltpu.VMEM((tt * bb, d_b), F32)]


def _relayout_scratch(tt, batch, d_model):
    return [pltpu.VMEM((2, tt, batch, d_model), F32), pltpu.VMEM((2, tt, batch, d_model), F32),
            pltpu.SemaphoreType.DMA((2, batch)), pltpu.SemaphoreType.DMA((2, batch))]


def _mixer_ab_stateful(x, states, w, bb, start_pos, name):
    batch, tt, d_model = x.shape
    d_a, d_b, conv_k = _ab_dims(w)
    state_specs = [_tm_state_spec(POOL_BUF, bb, d_a), _tm_state_spec(conv_k - 1, bb, d_b)]
    weights, weight_specs = _resident(_ab_weights(w))
    return pl.pallas_call(
        functools.partial(_mixer_ab_stateful_kernel, tt, bb, start_pos),
        grid=(batch // bb, 1),
        in_specs=[pl.BlockSpec(memory_space=pl.ANY)] + state_specs + weight_specs,
        out_specs=[_tm_spec(tt, bb, d_model)] + state_specs,
        out_shape=[jax.ShapeDtypeStruct((tt, batch, d_model), F32),
                   jax.ShapeDtypeStruct((POOL_BUF, batch, d_a), F32),
                   jax.ShapeDtypeStruct((conv_k - 1, batch, d_b), F32)],
        scratch_shapes=([pltpu.VMEM((tt, bb, d_model), F32), pltpu.SemaphoreType.DMA((tt,))]
                        + _ab_scratch(tt, bb, d_a, d_b, conv_k)),
        compiler_params=_PARAMS,
        name=name,
    )(x, *states, *weights)


def _mixer_ab_relayout(x, w, tt, name):
    batch, t_len, d_model = x.shape
    d_a, d_b, conv_k = _ab_dims(w)
    assert batch == SUBLANES and t_len % tt == 0 and tt >= conv_k - 1
    weights, weight_specs = _resident(_ab_weights(w))
    return pl.pallas_call(
        functools.partial(_mixer_ab_relayout_kernel, tt, batch),
        grid=(1, t_len // tt),
        in_specs=[pl.BlockSpec(memory_space=pl.ANY)] + weight_specs,
        out_specs=[pl.BlockSpec(memory_space=pl.ANY), _tm_state_spec(POOL_BUF, batch, d_a),
                   _tm_state_spec(conv_k - 1, batch, d_b)],
        out_shape=[jax.ShapeDtypeStruct(x.shape, F32),
                   jax.ShapeDtypeStruct((POOL_BUF, batch, d_a), F32),
                   jax.ShapeDtypeStruct((conv_k - 1, batch, d_b), F32)],
        scratch_shapes=_relayout_scratch(tt, batch, d_model) + _ab_scratch(tt, batch, d_a, d_b, conv_k),
        compiler_params=_PARAMS,
        name=name,
    )(x, *weights)


def _ffn_scratch(tt, bb, d_model, d_ff, n_keep):
    rows = tt * bb
    return [pltpu.VMEM((rows, d_model), BF16),
            pltpu.VMEM((rows, d_model), F32),
            pltpu.VMEM((rows + n_keep * bb, FF_CHUNK), F32),
            pltpu.VMEM((rows + n_keep * bb, FF_CHUNK), F32),
            pltpu.VMEM((n_keep * bb, 2 * d_ff), F32)]


def _ffn_weights(w):
    return (w["ffn_norm"], w["w_up"], w["ffn_conv_w"], w["ffn_conv_b"], w["w_down"], w["final_norm"])


def _ffn_stateful(x, states, w, layer, bb, final_norm, name):
    tt, batch, d_model = x.shape
    d_ff = _layer_shape(w["w_down"])[0]
    n_keep = _layer_shape(w["ffn_conv_w"])[0] - 1
    hbm = pl.BlockSpec(memory_space=pl.ANY)
    weights, weight_specs = _resident(_ffn_weights(w))
    state_bufs = [pltpu.VMEM((n_keep, bb, 2 * d_ff), F32), pltpu.VMEM((n_keep, bb, 2 * d_ff), F32),
                  pltpu.SemaphoreType.DMA((n_keep,)), pltpu.SemaphoreType.DMA((n_keep,))]
    y_bufs = [pltpu.VMEM((tt, bb, d_model), F32), pltpu.SemaphoreType.DMA((tt,))] if final_norm else []
    x_out_spec = hbm if final_norm else _tm_spec(tt, bb, d_model)
    x_out_shape = (batch, tt, d_model) if final_norm else x.shape
    return pl.pallas_call(
        functools.partial(_ffn_stateful_kernel, tt, bb, layer, final_norm),
        grid=(batch // bb, 1),
        in_specs=[_tm_spec(tt, bb, d_model), hbm] + weight_specs,
        out_specs=[x_out_spec, hbm],
        out_shape=[jax.ShapeDtypeStruct(x_out_shape, F32),
                   jax.ShapeDtypeStruct((batch, n_keep, 2 * d_ff), F32)],
        scratch_shapes=state_bufs + y_bufs + _ffn_scratch(tt, bb, d_model, d_ff, n_keep),
        compiler_params=_PARAMS,
        name=name,
    )(x, states, *weights)


def _ffn_relayout(x, w, w_up_f32, w_down_f32, layer, tt, final_norm, name):
    batch, t_len, d_model = x.shape
    d_ff = w_down_f32.shape[1]
    n_keep = _layer_shape(w["ffn_conv_w"])[0] - 1
    assert batch == SUBLANES and t_len % tt == 0
    small, small_specs = _resident((w["ffn_norm"], w["ffn_conv_w"], w["ffn_conv_b"], w["final_norm"]))
    hbm = pl.BlockSpec(memory_space=pl.ANY)
    weight_bufs = [pltpu.VMEM((d_model, 2 * d_ff), BF16), pltpu.VMEM((d_ff, d_model), BF16),
                   pltpu.VMEM((2, d_model, FF_CHUNK), F32), pltpu.VMEM((2, FF_CHUNK, d_model), F32),
                   pltpu.SemaphoreType.DMA((4,))]
    return pl.pallas_call(
        functools.partial(_ffn_relayout_kernel, tt, batch, layer, final_norm),
        grid=(1, t_len // tt),
        in_specs=[hbm, hbm, hbm] + small_specs,
        out_specs=[hbm, _tm_state_spec(n_keep, batch, 2 * d_ff), hbm, hbm],
        out_shape=[jax.ShapeDtypeStruct(x.shape, F32),
                   jax.ShapeDtypeStruct((n_keep, batch, 2 * d_ff), F32),
                   jax.ShapeDtypeStruct((d_model, 2 * d_ff), BF16),
                   jax.ShapeDtypeStruct((d_ff, d_model), BF16)],
        scratch_shapes=(_relayout_scratch(tt, batch, d_model) + weight_bufs
                        + _ffn_scratch(tt, batch, d_model, d_ff, n_keep)),
        compiler_params=_PARAMS,
        name=name,
    )(x, w_up_f32, w_down_f32, *small)


def _mixer_c(x, w, tt, bb, name):
    d_c = _layer_shape(w["w_out_c"])[0]
    extra_scratch = []
    if bb == 1:
        assert tt % CHUNK == 0
        batch, t_len, d_model = x.shape
        body = _mixer_c_kernel
        x_spec = _bm_spec(1, tt, d_model)
        v_spec = pl.BlockSpec((1, CHUNK, d_c), lambda bi, ti: (bi, 0, 0))
        v_shape = (batch, CHUNK, d_c)
        sgu = (w["sgu_w"], w["sgu_b_chunk"])
    else:
        t_len, batch, d_model = x.shape
        assert t_len == tt
        body = _mixer_c_short_kernel
        x_spec = _tm_spec(tt, bb, d_model)
        v_spec = pl.BlockSpec(memory_space=pl.ANY)
        v_shape = (batch, t_len, d_c)
        sgu = (w["sgu_w_short"], w["sgu_b_short"])
        extra_scratch = [pltpu.VMEM((tt, bb, d_c), F32), pltpu.SemaphoreType.DMA((tt,))]
    rows = tt * bb
    weights, weight_specs = _resident(
        (w["mix_norm"], w["w_in_c"], w["sgu_norm_g"], w["sgu_norm_b"]) + sgu + (w["w_out_c"],))
    return pl.pallas_call(
        functools.partial(body, tt, bb),
        grid=(batch // bb, t_len // tt),
        in_specs=[x_spec] + weight_specs,
        out_specs=[x_spec, v_spec],
        out_shape=[jax.ShapeDtypeStruct(x.shape, F32), jax.ShapeDtypeStruct(v_shape, F32)],
        scratch_shapes=extra_scratch + [pltpu.VMEM((rows, d_c), F32),
                                        pltpu.VMEM((rows, d_c), BF16),
                                        pltpu.VMEM((rows, d_c), F32)],
        compiler_params=_PARAMS,
        name=name,
    )(x, *weights)


def _tm(a):
    return jnp.swapaxes(a, 0, 1)


def kernel(x_prompt, x_sample, state_pool, state_conv, state_ffn, mix_norm, w_in_ab, pool_w, pool_scale, conv_w, conv_b, conv_norm_g, conv_norm_b, w_out_ab, w_in_c, sgu_norm_g, sgu_norm_b, sgu_w, sgu_bias, w_out_c, ffn_norm, w_up, ffn_conv_w, ffn_conv_b, w_down, final_norm):
    dec_seq = x_sample.shape[1]
    n_heads = sgu_w.shape[1]
    d_c = w_out_c.shape[1]
    dh = d_c // n_heads
    row = lambda a: a.reshape(1, -1)
    rows_of = lambda a: a.reshape(a.shape[0], 1, -1)
    fin = row(final_norm)
    mix_n, ffn_n, ffn_cb = rows_of(mix_norm), rows_of(ffn_norm), rows_of(ffn_conv_b)
    ffn = lambda layer: dict(ffn_norm=(ffn_n, layer), ffn_conv_w=(ffn_conv_w, layer),
                             ffn_conv_b=(ffn_cb, layer), final_norm=fin)
    l0 = dict(mix_norm=(mix_n, 0), w_in_ab=w_in_ab[0].astype(BF16), pool_w=pool_w[0].astype(BF16),
              pool_scale=row(pool_scale[0]), conv_w=conv_w[0], conv_b=row(conv_b[0]),
              conv_norm_g=row(conv_norm_g[0]), conv_norm_b=row(conv_norm_b[0]),
              w_out_ab=w_out_ab[0].astype(BF16), **ffn(0))
    sw_short = jnp.repeat(jnp.transpose(sgu_w[0][:, :dec_seq, :dec_seq], (1, 2, 0)), dh, axis=2)
    l1 = dict(mix_norm=(mix_n, 1), w_in_c=w_in_c[0].astype(BF16),
              sgu_norm_g=row(sgu_norm_g[0]), sgu_norm_b=row(sgu_norm_b[0]),
              sgu_w=sgu_w[0], sgu_b_chunk=jnp.repeat(sgu_bias[0].T, dh, axis=1),
              sgu_w_short=sw_short.reshape(dec_seq * dec_seq, d_c),
              sgu_b_short=jnp.repeat(sgu_bias[0][:, :dec_seq].T, dh, axis=1),
              w_out_c=w_out_c[0].astype(BF16), **ffn(1))

    xp, pool_p, conv_p = _mixer_ab_relayout(x_prompt, l0, 128, "mixer_ab_prompt")
    xp, f0_p, l0["w_up"], l0["w_down"] = _ffn_relayout(xp, l0, w_up, w_down, 0, 64, False, "ffn0_prompt")
    xp, v_p = _mixer_c(xp, l1, 512, 1, "mixer_c_prompt")
    yp, f1_p, l1["w_up"], l1["w_down"] = _ffn_relayout(xp, l1, w_up, w_down, 1, 64, True, "ffn1_prompt")

    bb_s = 64
    xs, pool_s, conv_s = _mixer_ab_stateful(x_sample, (_tm(state_pool[0]), _tm(state_conv[0])),
                                            l0, bb_s, PAST_LEN, "mixer_ab_sample")
    xs, f0_s = _ffn_stateful(xs, state_ffn, l0, 0, bb_s, False, "ffn0_sample")
    xs, v_s = _mixer_c(xs, l1, dec_seq, bb_s, "mixer_c_sample")
    ys, f1_s = _ffn_stateful(xs, state_ffn, l1, 1, bb_s, True, "ffn1_sample")

    return (yp, ys, _tm(pool_p)[None], _tm(pool_s)[None], _tm(conv_p)[None], _tm(conv_s)[None],
            v_p[None], v_s[None],
            jnp.stack([_tm(f0_p), _tm(f1_p)]), jnp.stack([f0_s, f1_s]))
```

```python
import functools

import jax
import jax.numpy as jnp
from jax import lax
from jax.experimental import pallas as pl
from jax.experimental.pallas import tpu as pltpu

F32 = jnp.float32
BF16 = jnp.bfloat16

EPS = 1e-6
POOL_WINDOWS = (2, 4, 8, 16)
POOL_BUF = max(POOL_WINDOWS) - 1
CHUNK = 128
PAST_LEN = 16384
LANES = 128
SUBLANES = 8
FF_CHUNK = 256
DOWN_GROUP = 4
UP_AHEAD = 3
CONV_ROWS = 64
VMEM_LIMIT = 56 * 1024 * 1024


def _rmsnorm(x, g):
    ms = jnp.mean(x * x, axis=-1, keepdims=True)
    return x * lax.rsqrt(ms + EPS) * g


def _layernorm(x, g, b):
    mu = jnp.mean(x, axis=-1, keepdims=True)
    xc = x - mu
    var = jnp.mean(xc * xc, axis=-1, keepdims=True)
    return xc * lax.rsqrt(var + EPS) * g + b


def _silu(x):
    return x * jax.nn.sigmoid(x)


def _bdot(a, b):
    return jnp.dot(a, b, preferred_element_type=F32)


def _ab_front(h, win_ref, pw_ref, ps_ref, xa_full, bb, pos0):
    rows = h.shape[0]
    d_a = xa_full.shape[1]
    hp = POOL_BUF * bb
    proj = _bdot(h, win_ref[...])
    d_b = (proj.shape[1] - d_a) // 2
    xa = proj[:, :d_a]
    glu = proj[:, d_a:d_a + d_b] * jax.nn.sigmoid(proj[:, d_a + d_b:])
    xa_full[hp:hp + rows, :] = xa
    pos = pos0 + lax.broadcasted_iota(jnp.int32, (rows, LANES), 0) // bb
    group = d_a // len(POOL_WINDOWS)
    ya = []
    for gi, w in enumerate(POOL_WINDOWS):
        c0 = gi * group
        xg = xa[:, c0:c0 + group]
        win = xg
        for k in range(1, w):
            win = win + xa_full[hp - k * bb:hp - k * bb + rows, c0:c0 + group]
        cnt = jnp.minimum(pos + 1, w).astype(F32)
        d = (win / cnt - xg).astype(BF16)
        ya.append((_bdot(d, pw_ref[gi]) * ps_ref[:, c0:c0 + group]).astype(BF16))
    return glu, jnp.concatenate(ya, axis=1)


def _conv_tile_dense(glu_full, c_ref, r0, lanes, taps, bias):
    n_out = CONV_ROWS // SUBLANES
    acc = [bias] * n_out
    for m in range(n_out + len(taps) - 1):
        grp = glu_full[pl.ds(r0 + m * SUBLANES, SUBLANES), lanes]
        for k in range(len(taps)):
            if 0 <= m - k < n_out:
                acc[m - k] = acc[m - k] + grp * taps[k]
    for i in range(n_out):
        c_ref[pl.ds(r0 + i * SUBLANES, SUBLANES), lanes] = acc[i]


def _depthwise_conv(glu_full, c_ref, rows, bb, cw_ref, cb_ref):
    n_taps = cw_ref.shape[0]
    n_blocks = rows // CONV_ROWS
    if bb != SUBLANES:
        def block(r, carry):
            r0 = pl.multiple_of(r * CONV_ROWS, CONV_ROWS)
            acc = jnp.broadcast_to(cb_ref[...], (CONV_ROWS, c_ref.shape[1]))
            for k in range(n_taps):
                acc = acc + glu_full[pl.ds(r0 + k * bb, CONV_ROWS), :] * cw_ref[k:k + 1, :]
            c_ref[pl.ds(r0, CONV_ROWS), :] = acc
            return carry

        lax.fori_loop(0, n_blocks, block, 0)
        return
    for lane0 in range(0, c_ref.shape[1], LANES):
        lanes = slice(lane0, lane0 + LANES)
        taps = [jnp.broadcast_to(cw_ref[k:k + 1, lanes], (SUBLANES, LANES)) for k in range(n_taps)]
        bias = jnp.broadcast_to(cb_ref[:, lanes], (SUBLANES, LANES))

        def block(r, carry, lanes=lanes, taps=taps, bias=bias):
            r0 = pl.multiple_of(r * CONV_ROWS, CONV_ROWS)
            _conv_tile_dense(glu_full, c_ref, r0, lanes, taps, bias)
            return carry

        lax.fori_loop(0, n_blocks, block, 0)


def _mixer_ab_kernel(tt, bb, has_state, start_pos, *refs):
    x_ref, refs = refs[0], refs[1:]
    if has_state:
        (sp_ref, sc_ref), refs = refs[:2], refs[2:]
    (g_ref, win_ref, pw_ref, ps_ref, cw_ref, cb_ref, lg_ref, lb_ref, wout_ref,
     xo_ref, po_ref, co_ref, xa_full, glu_full, c_ref) = refs
    rows = tt * bb
    hp = POOL_BUF * bb
    hc = (cw_ref.shape[0] - 1) * bb
    ti = pl.program_id(1)

    if has_state:
        xa_full[0:hp, :] = sp_ref[...].reshape(hp, xa_full.shape[1])
        glu_full[0:hc, :] = sc_ref[...].reshape(hc, glu_full.shape[1])
    else:
        @pl.when(ti == 0)
        def _():
            xa_full[...] = jnp.zeros(xa_full.shape, F32)
            glu_full[...] = jnp.zeros(glu_full.shape, F32)

        xa_full[0:hp, :] = xa_full[rows:rows + hp, :]
        glu_full[0:hc, :] = glu_full[rows:rows + hc, :]

    x = x_ref[...].reshape(rows, x_ref.shape[2])
    h = _rmsnorm(x, g_ref[...]).astype(BF16)
    glu, ya = _ab_front(h, win_ref, pw_ref, ps_ref, xa_full, bb, start_pos + ti * tt)
    glu_full[hc:hc + rows, :] = glu
    po_ref[...] = xa_full[rows:rows + hp, :].reshape(po_ref.shape)
    co_ref[...] = glu_full[rows:rows + hc, :].reshape(co_ref.shape)

    _depthwise_conv(glu_full, c_ref, rows, bb, cw_ref, cb_ref)
    yb = _silu(_layernorm(c_ref[...], lg_ref[...], lb_ref[...])).astype(BF16)
    out = x + _bdot(jnp.concatenate([ya, yb], axis=1), wout_ref[...])
    xo_ref[...] = out.reshape(xo_ref.shape)


def _mixer_ab_stateful_kernel(tt, bb, start_pos, x_hbm, *refs):
    rest, (x_buf, x_sem), scratch = refs[:14], refs[14:16], refs[16:]
    loads = _step_copies(x_hbm, x_buf, x_sem, pl.program_id(0) * bb, False)
    for c in loads:
        c.start()
    for c in loads:
        c.wait()
    _mixer_ab_kernel(tt, bb, True, start_pos, x_buf, *rest, *scratch)


def _ffn_conv_slabs(full_ref, up, halo_rows, fw, fb, bb, rows):
    halo = halo_rows.shape[0]
    full_ref[0:halo, :] = halo_rows
    full_ref[halo:halo + rows, :] = up
    tail = full_ref[rows:rows + halo, :]
    cv = (full_ref[0:rows, :] * fw[0:1, :] + full_ref[bb:bb + rows, :] * fw[1:2, :]
          + up * fw[2:3, :] + fb)
    return cv, tail


def _ffn_kernel(tt, bb, has_state, final_norm, *refs):
    x_ref, refs = refs[0], refs[1:]
    if has_state:
        st_ref, refs = refs[0], refs[1:]
    (g_ref, wup_ref, fw_ref, fb_ref, wdown_ref, fin_ref,
     xo_ref, so_ref, h_ref, acc_ref, full_a_ref, full_g_ref, carry_ref) = refs
    rows = tt * bb
    d_ff = wdown_ref.shape[0]
    n_keep = fw_ref.shape[0] - 1
    ti = pl.program_id(1)

    d_model = x_ref.shape[2]
    h_ref[...] = _rmsnorm(x_ref[...].reshape(rows, d_model), g_ref[...]).astype(BF16)

    if not has_state:
        @pl.when(ti == 0)
        def _():
            carry_ref[...] = jnp.zeros(carry_ref.shape, F32)

    def up_pair(c):
        return [_bdot(h_ref[...], wup_ref[:, off:off + FF_CHUNK])
                for off in (c * FF_CHUNK, d_ff + c * FF_CHUNK)]

    n_chunks = d_ff // FF_CHUNK
    ups_ahead = [up_pair(c) for c in range(UP_AHEAD)]
    acts = []
    for c in range(n_chunks):
        ups = ups_ahead.pop(0)
        if c + UP_AHEAD < n_chunks:
            ups_ahead.append(up_pair(c + UP_AHEAD))
        col_pair = [slice(off, off + FF_CHUNK) for off in (c * FF_CHUNK, d_ff + c * FF_CHUNK)]
        conv = []
        for full_ref, cols, up in zip((full_a_ref, full_g_ref), col_pair, ups):
            if has_state:
                halo_rows = st_ref[:, :, cols].reshape(n_keep * bb, FF_CHUNK)
            else:
                halo_rows = carry_ref[:, cols]
            cv, tail = _ffn_conv_slabs(full_ref, up, halo_rows, fw_ref[:, cols], fb_ref[:, cols], bb, rows)
            so_ref[:, :, cols] = tail.reshape(n_keep, bb, FF_CHUNK)
            if not has_state:
                carry_ref[:, cols] = tail
            conv.append(cv)
        acts.append((_silu(conv[0]) * conv[1]).astype(BF16))
        if len(acts) == DOWN_GROUP or c + 1 == n_chunks:
            lo = (c + 1 - len(acts)) * FF_CHUNK
            act = acts[0] if len(acts) == 1 else jnp.concatenate(acts, axis=1)
            base = x_ref[...].reshape(rows, d_model) if lo == 0 else acc_ref[...]
            total = base + _bdot(act, wdown_ref[lo:(c + 1) * FF_CHUNK, :])
            if c + 1 < n_chunks:
                acc_ref[...] = total
            else:
                if final_norm:
                    total = _rmsnorm(total, fin_ref[...])
                xo_ref[...] = total.reshape(xo_ref.shape)
            acts = []


def _tile_copies(hbm, buf, sems, slot, step, tt, to_hbm):
    copies = []
    for b in range(buf.shape[2]):
        rows_hbm = hbm.at[b, pl.ds(step * tt, tt), :]
        rows_vmem = buf.at[slot, :, b, :]
        src, dst = (rows_vmem, rows_hbm) if to_hbm else (rows_hbm, rows_vmem)
        copies.append(pltpu.make_async_copy(src, dst, sems.at[slot, b]))
    return copies


def _with_relayout(tt, x_hbm, xo_hbm, xbuf, obuf, in_sem, out_sem, compute):
    s = pl.program_id(1)
    n_steps = pl.num_programs(1)
    slot = s % 2

    def start(copies):
        for c in copies:
            c.start()

    def wait(copies):
        for c in copies:
            c.wait()

    @pl.when(s == 0)
    def _():
        start(_tile_copies(x_hbm, xbuf, in_sem, 0, 0, tt, False))

    @pl.when(s + 1 < n_steps)
    def _():
        start(_tile_copies(x_hbm, xbuf, in_sem, 1 - slot, s + 1, tt, False))

    wait(_tile_copies(x_hbm, xbuf, in_sem, slot, s, tt, False))

    @pl.when(s >= 2)
    def _():
        wait(_tile_copies(xo_hbm, obuf, out_sem, slot, s - 2, tt, True))

    compute(xbuf.at[slot], obuf.at[slot])
    start(_tile_copies(xo_hbm, obuf, out_sem, slot, s, tt, True))

    @pl.when(s == n_steps - 1)
    def _():
        @pl.when(s >= 1)
        def _():
            wait(_tile_copies(xo_hbm, obuf, out_sem, 1 - slot, s - 1, tt, True))
        wait(_tile_copies(xo_hbm, obuf, out_sem, slot, s, tt, True))


def _ffn_relayout_kernel(tt, bb, final_norm, x_hbm, *refs):
    weights, (xo_hbm, so_ref, xbuf, obuf, in_sem, out_sem), scratch = refs[:6], refs[6:12], refs[12:]
    _with_relayout(tt, x_hbm, xo_hbm, xbuf, obuf, in_sem, out_sem,
                   lambda x_tile, out_tile: _ffn_kernel(tt, bb, False, final_norm, x_tile, *weights,
                                                        out_tile, so_ref, *scratch))


def _mixer_ab_relayout_kernel(tt, bb, x_hbm, *refs):
    weights, (xo_hbm, po_ref, co_ref, xbuf, obuf, in_sem, out_sem), scratch = refs[:9], refs[9:16], refs[16:]
    _with_relayout(tt, x_hbm, xo_hbm, xbuf, obuf, in_sem, out_sem,
                   lambda x_tile, out_tile: _mixer_ab_kernel(tt, bb, False, 0, x_tile, *weights,
                                                             out_tile, po_ref, co_ref, *scratch))


def _step_copies(hbm, buf, sems, b0, to_hbm):
    copies = []
    for t in range(buf.shape[0]):
        rows_hbm = hbm.at[pl.ds(b0, buf.shape[1]), t, :]
        src, dst = (buf.at[t], rows_hbm) if to_hbm else (rows_hbm, buf.at[t])
        copies.append(pltpu.make_async_copy(src, dst, sems.at[t]))
    return copies


def _ffn_stateful_kernel(tt, bb, layer, final_norm, *refs):
    x_ref, st_hbm, refs = refs[0], refs[1], refs[2:]
    weights, (xo_ref, so_hbm, st_buf, so_buf, st_sem, so_sem), refs = refs[:6], refs[6:12], refs[12:]
    if final_norm:
        (y_buf, y_sem), refs = refs[:2], refs[2:]
    b0 = pl.program_id(0) * bb

    loads = _step_copies(st_hbm.at[layer], st_buf, st_sem, b0, False)
    for c in loads:
        c.start()
    for c in loads:
        c.wait()
    _ffn_kernel(tt, bb, True, final_norm, x_ref, st_buf, *weights,
                y_buf if final_norm else xo_ref, so_buf, *refs)
    stores = _step_copies(so_hbm, so_buf, so_sem, b0, True)
    if final_norm:
        stores += _step_copies(xo_ref, y_buf, y_sem, b0, True)
    for c in stores:
        c.start()
    for c in stores:
        c.wait()


def _mixer_c_kernel(tt, bb, *refs):
    (x_ref, g_ref, win_ref, lg_ref, lb_ref, sw_ref, sb_ref, wout_ref,
     xo_ref, vo_ref, v_ref, vb_ref, z_ref) = refs
    rows = tt * bb
    d_c = wout_ref.shape[0]

    x = x_ref[...].reshape(rows, x_ref.shape[2])
    h = _rmsnorm(x, g_ref[...]).astype(BF16)
    v = _layernorm(_bdot(h, win_ref[:, d_c:]), lg_ref[...], lb_ref[...])
    u = _bdot(h, win_ref[:, :d_c])

    if bb == 1:
        n_heads = sw_ref.shape[0]
        dh = d_c // n_heads
        n_chunks = tt // CHUNK
        vo_ref[0] = v[rows - CHUNK:, :]
        vb_ref[...] = v.astype(BF16)
        row_i = lax.broadcasted_iota(jnp.int32, (CHUNK, CHUNK), 0)
        col_i = lax.broadcasted_iota(jnp.int32, (CHUNK, CHUNK), 1)
        for hd in range(n_heads):
            hcols = slice(hd * dh, (hd + 1) * dh)
            m = jnp.where(col_i <= row_i, sw_ref[hd], 0.0).astype(BF16)
            rhs = jnp.concatenate(
                [vb_ref[c * CHUNK:(c + 1) * CHUNK, hcols] for c in range(n_chunks)], axis=1)
            zc = _bdot(m, rhs)
            bias = sb_ref[:, hcols]
            for c in range(n_chunks):
                z_ref[c * CHUNK:(c + 1) * CHUNK, hcols] = zc[:, c * dh:(c + 1) * dh] + bias
    else:
        vo_ref[...] = v.reshape(vo_ref.shape)
        v_ref[...] = v
        for i in range(tt):
            zi = jnp.broadcast_to(sb_ref[i:i + 1, :], (bb, d_c))
            for j in range(i + 1):
                zi = zi + v_ref[j * bb:(j + 1) * bb, :] * sw_ref[i * tt + j:i * tt + j + 1, :]
            z_ref[i * bb:(i + 1) * bb, :] = zi

    gated = (u * z_ref[...]).astype(BF16)
    out = x + _bdot(gated, wout_ref[...])
    xo_ref[...] = out.reshape(xo_ref.shape)


def _mixer_c_short_kernel(tt, bb, *refs):
    ins, (xo_ref, v_hbm, v_buf, v_sem), scratch = refs[:8], refs[8:12], refs[12:]
    _mixer_c_kernel(tt, bb, *ins, xo_ref, v_buf, *scratch)
    stores = _step_copies(v_hbm, v_buf, v_sem, pl.program_id(0) * bb, True)
    for c in stores:
        c.start()
    for c in stores:
        c.wait()


def _layer_shape(entry):
    return entry[0].shape[1:] if isinstance(entry, tuple) else entry.shape


def _resident(entries):
    arrays, specs = [], []
    for entry in entries:
        arr, layer = entry if isinstance(entry, tuple) else (entry, None)
        if layer is None:
            block, index = arr.shape, (0,) * arr.ndim
        else:
            block, index = (None,) + arr.shape[1:], (layer,) + (0,) * (arr.ndim - 1)
        arrays.append(arr)
        specs.append(pl.BlockSpec(block, lambda bi, ti, index=index: index,
                                  pipeline_mode=pl.Buffered(1)))
    return arrays, specs


def _tm_spec(tt, bb, width):
    return pl.BlockSpec((tt, bb, width), lambda bi, ti: (ti, bi, 0))


def _bm_spec(bb, tt, width):
    return pl.BlockSpec((bb, tt, width), lambda bi, ti: (bi, ti, 0))


def _tm_state_spec(n, bb, width):
    return pl.BlockSpec((n, bb, width), lambda bi, ti: (0, bi, 0))


_PARAMS = pltpu.CompilerParams(
    dimension_semantics=("arbitrary", "arbitrary"), vmem_limit_bytes=VMEM_LIMIT)


def _ab_weights(w):
    return (w["mix_norm"], w["w_in_ab"], w["pool_w"], w["pool_scale"], w["conv_w"], w["conv_b"],
            w["conv_norm_g"], w["conv_norm_b"], w["w_out_ab"])


def _ab_dims(w):
    return (_layer_shape(w["pool_scale"])[1], _layer_shape(w["conv_b"])[1], _layer_shape(w["conv_w"])[0])


def _ab_scratch(tt, bb, d_a, d_b, conv_k):
    return [pltpu.VMEM(((tt + POOL_BUF) * bb, d_a), F32),
            pltpu.VMEM(((tt + conv_k - 1) * bb, d_b), F32),
            pltpu.VMEM((tt * bb, d_b), F32)]


def _relayout_scratch(tt, batch, d_model):
    return [pltpu.VMEM((2, tt, batch, d_model), F32), pltpu.VMEM((2, tt, batch, d_model), F32),
            pltpu.SemaphoreType.DMA((2, batch)), pltpu.SemaphoreType.DMA((2, batch))]


def _mixer_ab_stateful(x, states, w, bb, start_pos, name):
    batch, tt, d_model = x.shape
    d_a, d_b, conv_k = _ab_dims(w)
    state_specs = [_tm_state_spec(POOL_BUF, bb, d_a), _tm_state_spec(conv_k - 1, bb, d_b)]
    weights, weight_specs = _resident(_ab_weights(w))
    return pl.pallas_call(
        functools.partial(_mixer_ab_stateful_kernel, tt, bb, start_pos),
        grid=(batch // bb, 1),
        in_specs=[pl.BlockSpec(memory_space=pl.ANY)] + state_specs + weight_specs,
        out_specs=[_tm_spec(tt, bb, d_model)] + state_specs,
        out_shape=[jax.ShapeDtypeStruct((tt, batch, d_model), F32),
                   jax.ShapeDtypeStruct((POOL_BUF, batch, d_a), F32),
                   jax.ShapeDtypeStruct((conv_k - 1, batch, d_b), F32)],
        scratch_shapes=([pltpu.VMEM((tt, bb, d_model), F32), pltpu.SemaphoreType.DMA((tt,))]
                        + _ab_scratch(tt, bb, d_a, d_b, conv_k)),
        compiler_params=_PARAMS,
        name=name,
    )(x, *states, *weights)


def _mixer_ab_relayout(x, w, tt, name):
    batch, t_len, d_model = x.shape
    d_a, d_b, conv_k = _ab_dims(w)
    assert batch == SUBLANES and t_len % tt == 0 and tt >= conv_k - 1
    weights, weight_specs = _resident(_ab_weights(w))
    return pl.pallas_call(
        functools.partial(_mixer_ab_relayout_kernel, tt, batch),
        grid=(1, t_len // tt),
        in_specs=[pl.BlockSpec(memory_space=pl.ANY)] + weight_specs,
        out_specs=[pl.BlockSpec(memory_space=pl.ANY), _tm_state_spec(POOL_BUF, batch, d_a),
                   _tm_state_spec(conv_k - 1, batch, d_b)],
        out_shape=[jax.ShapeDtypeStruct(x.shape, F32),
                   jax.ShapeDtypeStruct((POOL_BUF, batch, d_a), F32),
                   jax.ShapeDtypeStruct((conv_k - 1, batch, d_b), F32)],
        scratch_shapes=_relayout_scratch(tt, batch, d_model) + _ab_scratch(tt, batch, d_a, d_b, conv_k),
        compiler_params=_PARAMS,
        name=name,
    )(x, *weights)


def _ffn_scratch(tt, bb, d_model, d_ff, n_keep):
    rows = tt * bb
    return [pltpu.VMEM((rows, d_model), BF16),
            pltpu.VMEM((rows, d_model), F32),
            pltpu.VMEM((rows + n_keep * bb, FF_CHUNK), F32),
            pltpu.VMEM((rows + n_keep * bb, FF_CHUNK), F32),
            pltpu.VMEM((n_keep * bb, 2 * d_ff), F32)]


def _ffn_weights(w):
    return (w["ffn_norm"], w["w_up"], w["ffn_conv_w"], w["ffn_conv_b"], w["w_down"], w["final_norm"])


def _ffn_stateful(x, states, w, layer, bb, final_norm, name):
    tt, batch, d_model = x.shape
    d_ff = _layer_shape(w["w_down"])[0]
    n_keep = _layer_shape(w["ffn_conv_w"])[0] - 1
    hbm = pl.BlockSpec(memory_space=pl.ANY)
    weights, weight_specs = _resident(_ffn_weights(w))
    state_bufs = [pltpu.VMEM((n_keep, bb, 2 * d_ff), F32), pltpu.VMEM((n_keep, bb, 2 * d_ff), F32),
                  pltpu.SemaphoreType.DMA((n_keep,)), pltpu.SemaphoreType.DMA((n_keep,))]
    y_bufs = [pltpu.VMEM((tt, bb, d_model), F32), pltpu.SemaphoreType.DMA((tt,))] if final_norm else []
    x_out_spec = hbm if final_norm else _tm_spec(tt, bb, d_model)
    x_out_shape = (batch, tt, d_model) if final_norm else x.shape
    return pl.pallas_call(
        functools.partial(_ffn_stateful_kernel, tt, bb, layer, final_norm),
        grid=(batch // bb, 1),
        in_specs=[_tm_spec(tt, bb, d_model), hbm] + weight_specs,
        out_specs=[x_out_spec, hbm],
        out_shape=[jax.ShapeDtypeStruct(x_out_shape, F32),
                   jax.ShapeDtypeStruct((batch, n_keep, 2 * d_ff), F32)],
        scratch_shapes=state_bufs + y_bufs + _ffn_scratch(tt, bb, d_model, d_ff, n_keep),
        compiler_params=_PARAMS,
        name=name,
    )(x, states, *weights)


def _ffn_relayout(x, w, tt, final_norm, name):
    batch, t_len, d_model = x.shape
    d_ff = _layer_shape(w["w_down"])[0]
    n_keep = _layer_shape(w["ffn_conv_w"])[0] - 1
    assert batch == SUBLANES and t_len % tt == 0
    weights, weight_specs = _resident(_ffn_weights(w))
    return pl.pallas_call(
        functools.partial(_ffn_relayout_kernel, tt, batch, final_norm),
        grid=(1, t_len // tt),
        in_specs=[pl.BlockSpec(memory_space=pl.ANY)] + weight_specs,
        out_specs=[pl.BlockSpec(memory_space=pl.ANY), _tm_state_spec(n_keep, batch, 2 * d_ff)],
        out_shape=[jax.ShapeDtypeStruct(x.shape, F32),
                   jax.ShapeDtypeStruct((n_keep, batch, 2 * d_ff), F32)],
        scratch_shapes=(_relayout_scratch(tt, batch, d_model)
                        + _ffn_scratch(tt, batch, d_model, d_ff, n_keep)),
        compiler_params=_PARAMS,
        name=name,
    )(x, *weights)


def _mixer_c(x, w, tt, bb, name):
    d_c = _layer_shape(w["w_out_c"])[0]
    extra_scratch = []
    if bb == 1:
        assert tt % CHUNK == 0
        batch, t_len, d_model = x.shape
        body = _mixer_c_kernel
        x_spec = _bm_spec(1, tt, d_model)
        v_spec = pl.BlockSpec((1, CHUNK, d_c), lambda bi, ti: (bi, 0, 0))
        v_shape = (batch, CHUNK, d_c)
        sgu = (w["sgu_w"], w["sgu_b_chunk"])
    else:
        t_len, batch, d_model = x.shape
        assert t_len == tt
        body = _mixer_c_short_kernel
        x_spec = _tm_spec(tt, bb, d_model)
        v_spec = pl.BlockSpec(memory_space=pl.ANY)
        v_shape = (batch, t_len, d_c)
        sgu = (w["sgu_w_short"], w["sgu_b_short"])
        extra_scratch = [pltpu.VMEM((tt, bb, d_c), F32), pltpu.SemaphoreType.DMA((tt,))]
    rows = tt * bb
    weights, weight_specs = _resident(
        (w["mix_norm"], w["w_in_c"], w["sgu_norm_g"], w["sgu_norm_b"]) + sgu + (w["w_out_c"],))
    return pl.pallas_call(
        functools.partial(body, tt, bb),
        grid=(batch // bb, t_len // tt),
        in_specs=[x_spec] + weight_specs,
        out_specs=[x_spec, v_spec],
        out_shape=[jax.ShapeDtypeStruct(x.shape, F32), jax.ShapeDtypeStruct(v_shape, F32)],
        scratch_shapes=extra_scratch + [pltpu.VMEM((rows, d_c), F32),
                                        pltpu.VMEM((rows, d_c), BF16),
                                        pltpu.VMEM((rows, d_c), F32)],
        compiler_params=_PARAMS,
        name=name,
    )(x, *weights)


def _tm(a):
    return jnp.swapaxes(a, 0, 1)


def kernel(x_prompt, x_sample, state_pool, state_conv, state_ffn, mix_norm, w_in_ab, pool_w, pool_scale, conv_w, conv_b, conv_norm_g, conv_norm_b, w_out_ab, w_in_c, sgu_norm_g, sgu_norm_b, sgu_w, sgu_bias, w_out_c, ffn_norm, w_up, ffn_conv_w, ffn_conv_b, w_down, final_norm):
    dec_seq = x_sample.shape[1]
    n_heads = sgu_w.shape[1]
    d_c = w_out_c.shape[1]
    dh = d_c // n_heads
    row = lambda a: a.reshape(1, -1)
    rows_of = lambda a: a.reshape(a.shape[0], 1, -1)
    fin = row(final_norm)
    mix_n, ffn_n, ffn_cb = rows_of(mix_norm), rows_of(ffn_norm), rows_of(ffn_conv_b)
    w_up_b, w_down_b = w_up.astype(BF16), w_down.astype(BF16)
    ffn = lambda layer: dict(ffn_norm=(ffn_n, layer), w_up=(w_up_b, layer), ffn_conv_w=(ffn_conv_w, layer),
                             ffn_conv_b=(ffn_cb, layer), w_down=(w_down_b, layer), final_norm=fin)
    l0 = dict(mix_norm=(mix_n, 0), w_in_ab=w_in_ab[0].astype(BF16), pool_w=pool_w[0].astype(BF16),
              pool_scale=row(pool_scale[0]), conv_w=conv_w[0], conv_b=row(conv_b[0]),
              conv_norm_g=row(conv_norm_g[0]), conv_norm_b=row(conv_norm_b[0]),
              w_out_ab=w_out_ab[0].astype(BF16), **ffn(0))
    sw_short = jnp.repeat(jnp.transpose(sgu_w[0][:, :dec_seq, :dec_seq], (1, 2, 0)), dh, axis=2)
    l1 = dict(mix_norm=(mix_n, 1), w_in_c=w_in_c[0].astype(BF16),
              sgu_norm_g=row(sgu_norm_g[0]), sgu_norm_b=row(sgu_norm_b[0]),
              sgu_w=sgu_w[0], sgu_b_chunk=jnp.repeat(sgu_bias[0].T, dh, axis=1),
              sgu_w_short=sw_short.reshape(dec_seq * dec_seq, d_c),
              sgu_b_short=jnp.repeat(sgu_bias[0][:, :dec_seq].T, dh, axis=1),
              w_out_c=w_out_c[0].astype(BF16), **ffn(1))

    xp, pool_p, conv_p = _mixer_ab_relayout(x_prompt, l0, 128, "mixer_ab_prompt")
    xp, f0_p = _ffn_relayout(xp, l0, 128, False, "ffn0_prompt")
    xp, v_p = _mixer_c(xp, l1, 512, 1, "mixer_c_prompt")
    yp, f1_p = _ffn_relayout(xp, l1, 128, True, "ffn1_prompt")

    bb_s = 64
    xs, pool_s, conv_s = _mixer_ab_stateful(x_sample, (_tm(state_pool[0]), _tm(state_conv[0])),
                                            l0, bb_s, PAST_LEN, "mixer_ab_sample")
    xs, f0_s = _ffn_stateful(xs, state_ffn, l0, 0, bb_s, False, "ffn0_sample")
    xs, v_s = _mixer_c(xs, l1, dec_seq, bb_s, "mixer_c_sample")
    ys, f1_s = _ffn_stateful(xs, state_ffn, l1, 1, bb_s, True, "ffn1_sample")

    return (yp, ys, _tm(pool_p)[None], _tm(pool_s)[None], _tm(conv_p)[None], _tm(conv_s)[None],
            v_p[None], v_s[None],
            jnp.stack([_tm(f0_p), _tm(f1_p)]), jnp.stack([f0_s, f1_s]))
```
